```python
import jax, jax.numpy as jnp
from jax import lax
import numpy as np

D_MODEL = 1024
BATCH = 8
SEQ = 8192
DEPTH = 1

D_FF = 2816
D_MIX = D_MODEL
N_HEADS = 8
HEAD_DIM = 64
N_KV_HEADS = 2
KV_GROUP = N_HEADS // N_KV_HEADS
D_ATTN = N_HEADS * HEAD_DIM
D_KV = N_KV_HEADS * HEAD_DIM
Q_BLOCK = 128
ROPE_THETA = 10000.0
AXIS_DIM = HEAD_DIM // 2
GRID_W = 64
N_SGU_GROUPS = 8
SGU_GROUP_DIM = 64
D_SGU = N_SGU_GROUPS * SGU_GROUP_DIM
CHUNK = 128
D_IN = D_ATTN + 2 * D_KV + 2 * D_SGU
EPS = 1e-6

kernel_name = "hybrid_macaron_gmlp_gqa_axial_encoder_block"


def rms_norm(x, g):
    xf = x.astype(jnp.float32)
    y = xf * lax.rsqrt(jnp.mean(xf * xf, axis=-1, keepdims=True) + EPS)
    return (y * g.astype(jnp.float32)).astype(x.dtype)


def swiglu(h, w_gate, w_up, w_down):
    return (jax.nn.silu(h @ w_gate) * (h @ w_up)) @ w_down


def axial_rope_tables(rows):
    row_idx = jnp.repeat(jnp.arange(rows, dtype=jnp.float32), GRID_W)
    col_idx = jnp.tile(jnp.arange(GRID_W, dtype=jnp.float32), rows)
    inv = 1.0 / (ROPE_THETA ** (jnp.arange(0, AXIS_DIM, 2, dtype=jnp.float32) / AXIS_DIM))
    ang = jnp.concatenate([row_idx[:, None] * inv, col_idx[:, None] * inv], axis=-1)
    return jnp.cos(ang), jnp.sin(ang)


def apply_rope(x, cos, sin):
    b, s, h, d = x.shape
    xf = x.astype(jnp.float32).reshape(b, s, h, d // 2, 2)
    x1, x2 = xf[..., 0], xf[..., 1]
    c = cos[None, :, None, :]
    sn = sin[None, :, None, :]
    out = jnp.stack([x1 * c - x2 * sn, x1 * sn + x2 * c], axis=-1)
    return out.reshape(b, s, h, d).astype(x.dtype)


def gqa_attention(q, k, v):
    b, s, _, d = q.shape
    nblk = s // Q_BLOCK
    scale = HEAD_DIM ** -0.5
    qb = q.reshape(b, nblk, Q_BLOCK, N_KV_HEADS, KV_GROUP, d).transpose(1, 0, 2, 3, 4, 5)

    def one_block(qi):
        sc = jnp.einsum('bqkgd,bskd->bkgqs', qi, k, preferred_element_type=jnp.float32) * scale
        p = jax.nn.softmax(sc, axis=-1)
        return jnp.einsum('bkgqs,bskd->bqkgd', p.astype(v.dtype), v)

    o = lax.map(one_block, qb)
    return o.transpose(1, 0, 2, 3, 4, 5).reshape(b, s, N_HEADS * d)


def spatial_gating(z, g_sgu, w_s, b_s):
    b, s, _ = z.shape
    u, vv = jnp.split(z, 2, axis=-1)
    vv = rms_norm(vv, g_sgu)
    vv = vv.reshape(b, s // CHUNK, CHUNK, N_SGU_GROUPS, SGU_GROUP_DIM)
    f = jnp.einsum('gpq,bnqgd->bnpgd', w_s, vv) + b_s.T[None, None, :, :, None]
    return u * f.reshape(b, s, D_SGU)


def setup_inputs(seed: int = 0) -> dict:
    key = jax.random.key(seed)
    ks = jax.random.split(key, 24)
    L = DEPTH
    nrm = lambda k, shape, fan_in: jax.random.normal(k, shape, jnp.float32) * fan_in ** -0.5
    gain = lambda k, shape: 1.0 + 0.02 * jax.random.normal(k, shape, jnp.float32)
    return {
        "x": jax.random.normal(ks[0], (BATCH, SEQ, D_MODEL), jnp.float32),
        "g_ffn1": gain(ks[1], (L, D_MODEL)),
        "w1_gate": nrm(ks[2], (L, D_MODEL, D_FF), D_MODEL),
        "w1_up": nrm(ks[3], (L, D_MODEL, D_FF), D_MODEL),
        "w1_down": nrm(ks[4], (L, D_FF, D_MODEL), D_FF),
        "g_mix": gain(ks[5], (L, D_MODEL)),
        "w_in": nrm(ks[6], (L, D_MODEL, D_IN), D_MODEL),
        "g_q": gain(ks[7], (L, HEAD_DIM)),
        "g_k": gain(ks[8], (L, HEAD_DIM)),
        "g_sgu": gain(ks[9], (L, D_SGU)),
        "w_s": nrm(ks[10], (L, N_SGU_GROUPS, CHUNK, CHUNK), CHUNK),
        "b_s": 1.0 + 0.02 * jax.random.normal(ks[11], (L, N_SGU_GROUPS, CHUNK), jnp.float32),
        "g_attn_out": gain(ks[12], (L, D_ATTN)),
        "g_sgu_out": gain(ks[13], (L, D_SGU)),
        "w_out": nrm(ks[14], (L, D_MIX, D_MODEL), D_MIX),
        "g_ffn2": gain(ks[15], (L, D_MODEL)),
        "w2_gate": nrm(ks[16], (L, D_MODEL, D_FF), D_MODEL),
        "w2_up": nrm(ks[17], (L, D_MODEL, D_FF), D_MODEL),
        "w2_down": nrm(ks[18], (L, D_FF, D_MODEL), D_FF),
        "g_final": gain(ks[19], (L, D_MODEL)),
    }


def reference(x, g_ffn1, w1_gate, w1_up, w1_down, g_mix, w_in, g_q, g_k, g_sgu, w_s, b_s,
              g_attn_out, g_sgu_out, w_out, g_ffn2, w2_gate, w2_up, w2_down, g_final):
    b, s, _ = x.shape
    rows = s // GRID_W
    cos, sin = axial_rope_tables(rows)
    for l in range(DEPTH):
        x = x + 0.5 * swiglu(rms_norm(x, g_ffn1[l]), w1_gate[l], w1_up[l], w1_down[l])

        h = rms_norm(x, g_mix[l])
        proj = h @ w_in[l]
        q, k, v, z = jnp.split(proj, [D_ATTN, D_ATTN + D_KV, D_ATTN + 2 * D_KV], axis=-1)

        q = rms_norm(q.reshape(b, s, N_HEADS, HEAD_DIM), g_q[l])
        k = rms_norm(k.reshape(b, s, N_KV_HEADS, HEAD_DIM), g_k[l])
        v = v.reshape(b, s, N_KV_HEADS, HEAD_DIM)
        q = apply_rope(q, cos, sin)
        k = apply_rope(k, cos, sin)
        attn = gqa_attention(q, k, v)

        sgu = spatial_gating(jax.nn.gelu(z), g_sgu[l], w_s[l], b_s[l])

        mixed = jnp.concatenate([rms_norm(attn, g_attn_out[l]), rms_norm(sgu, g_sgu_out[l])], axis=-1)
        x = x + mixed @ w_out[l]

        x = x + 0.5 * swiglu(rms_norm(x, g_ffn2[l]), w2_gate[l], w2_up[l], w2_down[l])
        x = rms_norm(x, g_final[l])
    return x
```

```python
import functools

import jax
import jax.numpy as jnp
from jax import lax
from jax.experimental import pallas as pl
from jax.experimental.pallas import tpu as pltpu

F32 = jnp.float32
BF16 = jnp.bfloat16

D_MODEL = 1024
D_FF = 2816
N_HEADS = 8
HEAD_DIM = 64
N_KV_HEADS = 2
KV_GROUP = N_HEADS // N_KV_HEADS
D_ATTN = N_HEADS * HEAD_DIM
D_KV = N_KV_HEADS * HEAD_DIM
ROPE_THETA = 10000.0
AXIS_DIM = HEAD_DIM // 2
GRID_W = 64
N_SGU_GROUPS = 8
D_SGU = 512
CHUNK = 128
D_IN = D_ATTN + 2 * D_KV + 2 * D_SGU
EPS = 1e-6

LANES = 128
MXU_DIM = 256
VMEM_LIMIT_BYTES = 56 * 1024 * 1024

FF_CHUNK = MXU_DIM
N_FF_CHUNKS = D_FF // FF_CHUNK
TM = 512
TQ = 128
TK = 512


def _rms_norm(x, g):
    ms = jnp.mean(x * x, axis=-1, keepdims=True)
    return x * lax.rsqrt(ms + EPS) * g


def _swiglu_residual(x, g_ref, wgu_ref, wd_ref, hid_ref):
    xn = _rms_norm(x, g_ref[...]).astype(BF16)
    for c in range(N_FF_CHUNKS):
        h = jnp.dot(xn, wgu_ref[:, 2 * FF_CHUNK * c:2 * FF_CHUNK * (c + 1)],
                    preferred_element_type=F32)
        a = jax.nn.silu(h[:, :FF_CHUNK]) * h[:, FF_CHUNK:]
        hid_ref[:, FF_CHUNK * c:FF_CHUNK * (c + 1)] = a.astype(BF16)
    y = jnp.dot(hid_ref[...], wd_ref[...], preferred_element_type=F32)
    return x + 0.5 * y


def _head_mean_square(t, bmat):
    t2 = t * t
    hi = t2.astype(BF16)
    lo = (t2 - hi.astype(F32)).astype(BF16)
    return (jnp.dot(hi, bmat, preferred_element_type=F32)
            + jnp.dot(lo, bmat, preferred_element_type=F32))


def _rope_block(t, cos_t, sin_t, even_lane):
    nxt = pltpu.roll(t, LANES - 1, axis=1)
    prv = pltpu.roll(t, 1, axis=1)
    return t * cos_t + jnp.where(even_lane, nxt, prv) * sin_t


def _front_kernel(x_ref, g1_ref, wgu_ref, wd_ref, gmix_ref, win_ref, gq_ref, gk_ref,
                  bmat_ref, cos_ref, sin_ref, gsgu_ref, ws2_ref, bias_ref, gsout_ref,
                  x1_ref, q_ref, kt_ref, v_ref, sgu_ref, hid_ref, gate_ref):
    tm = x_ref.shape[0]
    x1 = _swiglu_residual(x_ref[...], g1_ref, wgu_ref, wd_ref, hid_ref)
    x1_ref[...] = x1

    h = _rms_norm(x1, gmix_ref[...]).astype(BF16)
    proj = jnp.dot(h, win_ref[...], preferred_element_type=F32)
    q = proj[:, :D_ATTN]
    k = proj[:, D_ATTN:D_ATTN + D_KV]
    v = proj[:, D_ATTN + D_KV:D_ATTN + 2 * D_KV]
    z = proj[:, D_ATTN + 2 * D_KV:]

    lane = lax.broadcasted_iota(jnp.int32, (tm, LANES), 1)
    even_lane = (lane % 2) == 0
    low_half = lane < HEAD_DIM
    cos_t = cos_ref[...]
    sin_t = sin_ref[...]
    bmat = bmat_ref[...]

    qn = q * lax.rsqrt(_head_mean_square(q, bmat) + EPS) * gq_ref[...]
    scale = HEAD_DIM ** -0.5
    for j in range(D_ATTN // LANES):
        blk = _rope_block(qn[:, LANES * j:LANES * (j + 1)], cos_t, sin_t, even_lane)
        q_ref[:, LANES * j:LANES * (j + 1)] = (blk * scale).astype(BF16)

    kn = k * lax.rsqrt(_head_mean_square(k, bmat[:D_KV, :D_KV]) + EPS) * gk_ref[...]
    kr_t = _rope_block(kn, cos_t, sin_t, even_lane).T.astype(BF16)
    for kv in range(N_KV_HEADS):
        rows = kr_t[HEAD_DIM * kv:HEAD_DIM * (kv + 1), :]
        kt_ref[LANES * kv:LANES * kv + HEAD_DIM, :] = rows
        kt_ref[LANES * kv + HEAD_DIM:LANES * (kv + 1), :] = rows

    v_sw = pltpu.roll(v, HEAD_DIM, axis=1)
    v_ref[:, :LANES] = jnp.where(low_half, v, 1.0).astype(BF16)
    v_ref[:, LANES:] = jnp.where(low_half, v_sw, 1.0).astype(BF16)

    zz = jax.nn.gelu(z)
    u = zz[:, :D_SGU]
    vvn = _rms_norm(zz[:, D_SGU:], gsgu_ref[...]).astype(BF16)
    low_c = lax.broadcasted_iota(jnp.int32, (CHUNK, LANES), 1) < HEAD_DIM
    for c in range(tm // CHUNK):
        for j in range(D_SGU // LANES):
            blk = vvn[CHUNK * c:CHUNK * (c + 1), LANES * j:LANES * (j + 1)]
            r = jnp.dot(ws2_ref[j], blk, preferred_element_type=F32)
            f = jnp.where(low_c, r[:CHUNK], r[CHUNK:]) + bias_ref[:, LANES * j:LANES * (j + 1)]
            gate_ref[CHUNK * c:CHUNK * (c + 1), LANES * j:LANES * (j + 1)] = (
                u[CHUNK * c:CHUNK * (c + 1), LANES * j:LANES * (j + 1)] * f)
    sgu_ref[...] = _rms_norm(gate_ref[...], gsout_ref[...]).astype(BF16)


def _attn_kernel(q_ref, kt_ref, v_ref, gout_ref, o_ref, qg_ref, m_ref, acc_ref):
    tq = q_ref.shape[0]
    n_kv_blocks = kt_ref.shape[1] // TK
    lane = lax.broadcasted_iota(jnp.int32, (tq, LANES), 1)
    low_half = lane < HEAD_DIM

    for h in range(N_HEADS):
        g, hh = divmod(h, KV_GROUP)
        blk = q_ref[:, LANES * (h // 2):LANES * (h // 2 + 1)]
        keep = low_half if h % 2 == 0 else jnp.logical_not(low_half)
        qg_ref[g, tq * hh:tq * (hh + 1), :] = jnp.where(keep, blk, jnp.zeros_like(blk))
    m_ref[...] = jnp.full(m_ref.shape, -jnp.inf, F32)
    acc_ref[...] = jnp.zeros(acc_ref.shape, F32)

    def kv_step(i, carry):
        off = pl.multiple_of(i * TK, TK)
        for g in range(N_KV_HEADS):
            kblk = kt_ref[LANES * g:LANES * (g + 1), pl.ds(off, TK)]
            s = jnp.dot(qg_ref[g], kblk, preferred_element_type=F32)
            m_old = m_ref[g]
            m_new = jnp.maximum(m_old, jnp.max(s, axis=-1, keepdims=True))
            alpha = jnp.exp(m_old - m_new)
            p = jnp.exp(s - m_new).astype(BF16)
            vblk = v_ref[pl.ds(off, TK), LANES * g:LANES * (g + 1)]
            acc_ref[g] = alpha * acc_ref[g] + jnp.dot(p, vblk, preferred_element_type=F32)
            m_ref[g] = m_new
        return carry

    lax.fori_loop(0, n_kv_blocks, kv_step, 0)

    blocks = []
    for j in range(N_HEADS // 2):
        outs = []
        for h in (2 * j, 2 * j + 1):
            g, hh = divmod(h, KV_GROUP)
            a = acc_ref[g, tq * hh:tq * (hh + 1), :]
            outs.append((a, pltpu.roll(a, HEAD_DIM, axis=1)))
        (a_e, r_e), (a_o, r_o) = outs
        blocks.append(jnp.where(low_half, a_e / r_e, r_o / a_o))
    attn = jnp.concatenate(blocks, axis=1)
    o_ref[...] = _rms_norm(attn, gout_ref[...]).astype(BF16)


def _back_kernel(x1_ref, attn_ref, sgu_ref, wo_a_ref, wo_s_ref, g2_ref, wgu_ref, wd_ref,
                 gfin_ref, o_ref, hid_ref):
    mix = (jnp.dot(attn_ref[...], wo_a_ref[...], preferred_element_type=F32)
           + jnp.dot(sgu_ref[...], wo_s_ref[...], preferred_element_type=F32))
    x2 = x1_ref[...] + mix
    x3 = _swiglu_residual(x2, g2_ref, wgu_ref, wd_ref, hid_ref)
    o_ref[...] = _rms_norm(x3, gfin_ref[...])


def _const_spec(shape):
    zeros = (0,) * len(shape)
    return pl.BlockSpec(shape, lambda *_: zeros, pipeline_mode=pl.Buffered(1))


def _interleave_gate_up(w_gate, w_up):
    d = w_gate.shape[0]
    g = w_gate.reshape(d, N_FF_CHUNKS, FF_CHUNK)
    u = w_up.reshape(d, N_FF_CHUNKS, FF_CHUNK)
    return jnp.concatenate([g, u], axis=-1).reshape(d, 2 * D_FF).astype(BF16)


def _rope_tables(seq):
    rows = seq // GRID_W
    row_idx = jnp.repeat(jnp.arange(rows, dtype=F32), GRID_W)
    col_idx = jnp.tile(jnp.arange(GRID_W, dtype=F32), rows)
    inv = 1.0 / (ROPE_THETA ** (jnp.arange(0, AXIS_DIM, 2, dtype=F32) / AXIS_DIM))
    ang = jnp.concatenate([row_idx[:, None] * inv, col_idx[:, None] * inv], axis=-1)
    cos = jnp.repeat(jnp.cos(ang), 2, axis=-1)
    sin = jnp.repeat(jnp.sin(ang), 2, axis=-1)
    sign = jnp.tile(jnp.array([-1.0, 1.0], F32), HEAD_DIM // 2)
    reps = LANES // HEAD_DIM
    return jnp.tile(cos, (1, reps)), jnp.tile(sin * sign, (1, reps))


def kernel(x, g_ffn1, w1_gate, w1_up, w1_down, g_mix, w_in, g_q, g_k, g_sgu, w_s, b_s,
           g_attn_out, g_sgu_out, w_out, g_ffn2, w2_gate, w2_up, w2_down, g_final):
    batch, seq, d_model = x.shape
    assert d_model == D_MODEL and seq % TM == 0 and seq % TK == 0 and seq % TQ == 0
    assert g_ffn1.shape[0] == 1, "single-layer block"
    n_tok = batch * seq
    tiles_per_seq = seq // TM

    row = lambda a: a.reshape(1, -1).astype(F32)
    wgu1 = _interleave_gate_up(w1_gate[0], w1_up[0])
    wgu2 = _interleave_gate_up(w2_gate[0], w2_up[0])
    wd1 = w1_down[0].astype(BF16)
    wd2 = w2_down[0].astype(BF16)
    win = w_in[0].astype(BF16)
    wo_a = w_out[0, :D_ATTN].astype(BF16)
    wo_s = w_out[0, D_ATTN:].astype(BF16)
    gq = jnp.tile(row(g_q[0]), (1, N_HEADS))
    gk = jnp.tile(row(g_k[0]), (1, N_KV_HEADS))
    head_id = jnp.arange(D_ATTN) // HEAD_DIM
    bmat = jnp.where(head_id[:, None] == head_id[None, :], 1.0 / HEAD_DIM, 0.0).astype(BF16)
    cos_t, sin_t = _rope_tables(seq)
    ws2 = w_s[0].reshape(N_SGU_GROUPS // 2, 2 * CHUNK, CHUNK).astype(BF16)
    bias = jnp.repeat(b_s[0].T, D_SGU // N_SGU_GROUPS, axis=1).astype(F32)

    params = pltpu.CompilerParams(dimension_semantics=("arbitrary",),
                                  vmem_limit_bytes=VMEM_LIMIT_BYTES)
    tok_spec = lambda width: pl.BlockSpec((TM, width), lambda i: (i, 0))
    pos_spec = pl.BlockSpec((TM, LANES), lambda i: (i % tiles_per_seq, 0))

    x1, q, kt, v2, sgu = pl.pallas_call(
        _front_kernel,
        grid=(n_tok // TM,),
        in_specs=[
            tok_spec(D_MODEL),
            _const_spec((1, D_MODEL)), _const_spec((D_MODEL, 2 * D_FF)), _const_spec((D_FF, D_MODEL)),
            _const_spec((1, D_MODEL)), _const_spec((D_MODEL, D_IN)),
            _const_spec((1, D_ATTN)), _const_spec((1, D_KV)), _const_spec((D_ATTN, D_ATTN)),
            pos_spec, pos_spec,
            _const_spec((1, D_SGU)), _const_spec((N_SGU_GROUPS // 2, 2 * CHUNK, CHUNK)),
            _const_spec((CHUNK, D_SGU)), _const_spec((1, D_SGU)),
        ],
        out_specs=[
            tok_spec(D_MODEL),
            tok_spec(D_ATTN),
            pl.BlockSpec((None, 2 * LANES, TM), lambda i: (i // tiles_per_seq, 0, i % tiles_per_seq)),
            tok_spec(2 * LANES),
            tok_spec(D_SGU),
        ],
        out_shape=[
            jax.ShapeDtypeStruct((n_tok, D_MODEL), F32),
            jax.ShapeDtypeStruct((n_tok, D_ATTN), BF16),
            jax.ShapeDtypeStruct((batch, 2 * LANES, seq), BF16),
            jax.ShapeDtypeStruct((n_tok, 2 * LANES), BF16),
            jax.ShapeDtypeStruct((n_tok, D_SGU), BF16),
        ],
        scratch_shapes=[pltpu.VMEM((TM, D_FF), BF16), pltpu.VMEM((TM, D_SGU), F32)],
        compiler_params=params,
        name="front_ffn_proj",
    )(x.reshape(n_tok, D_MODEL), row(g_ffn1[0]), wgu1, wd1, row(g_mix[0]), win, gq, gk, bmat,
      cos_t, sin_t, row(g_sgu[0]), ws2, bias, row(g_sgu_out[0]))

    attn = pl.pallas_call(
        _attn_kernel,
        grid=(batch, seq // TQ),
        in_specs=[
            pl.BlockSpec((None, TQ, D_ATTN), lambda b, i: (b, i, 0)),
            pl.BlockSpec((None, 2 * LANES, seq), lambda b, i: (b, 0, 0)),
            pl.BlockSpec((None, seq, 2 * LANES), lambda b, i: (b, 0, 0)),
            _const_spec((1, D_ATTN)),
        ],
        out_specs=pl.BlockSpec((None, TQ, D_ATTN), lambda b, i: (b, i, 0)),
        out_shape=jax.ShapeDtypeStruct((batch, seq, D_ATTN), BF16),
        scratch_shapes=[
            pltpu.VMEM((N_KV_HEADS, KV_GROUP * TQ, LANES), BF16),
            pltpu.VMEM((N_KV_HEADS, KV_GROUP * TQ, 1), F32),
            pltpu.VMEM((N_KV_HEADS, KV_GROUP * TQ, LANES), F32),
        ],
        compiler_params=pltpu.CompilerParams(dimension_semantics=("arbitrary", "arbitrary"),
                                             vmem_limit_bytes=VMEM_LIMIT_BYTES),
        name="gqa_attention",
    )(q.reshape(batch, seq, D_ATTN), kt, v2.reshape(batch, seq, 2 * LANES), row(g_attn_out[0]))

    out = pl.pallas_call(
        _back_kernel,
        grid=(n_tok // TM,),
        in_specs=[
            tok_spec(D_MODEL), tok_spec(D_ATTN), tok_spec(D_SGU),
            _const_spec((D_ATTN, D_MODEL)), _const_spec((D_SGU, D_MODEL)),
            _const_spec((1, D_MODEL)), _const_spec((D_MODEL, 2 * D_FF)), _const_spec((D_FF, D_MODEL)),
            _const_spec((1, D_MODEL)),
        ],
        out_specs=tok_spec(D_MODEL),
        out_shape=jax.ShapeDtypeStruct((n_tok, D_MODEL), F32),
        scratch_shapes=[pltpu.VMEM((TM, D_FF), BF16)],
        compiler_params=params,
        name="back_proj_ffn",
    )(x1, attn.reshape(n_tok, D_ATTN), sgu, wo_a, wo_s, row(g_ffn2[0]), wgu2, wd2, row(g_final[0]))

    return out.reshape(batch, seq, D_MODEL)
```

```python
import functools

import jax
import jax.numpy as jnp
from jax import lax
from jax.experimental import pallas as pl
from jax.experimental.pallas import tpu as pltpu

F32 = jnp.float32
BF16 = jnp.bfloat16

D_MODEL = 1024
D_FF = 2816
N_HEADS = 8
HEAD_DIM = 64
N_KV_HEADS = 2
KV_GROUP = N_HEADS // N_KV_HEADS
D_ATTN = N_HEADS * HEAD_DIM
D_KV = N_KV_HEADS * HEAD_DIM
ROPE_THETA = 10000.0
AXIS_DIM = HEAD_DIM // 2
GRID_W = 64
N_SGU_GROUPS = 8
D_SGU = 512
CHUNK = 128
D_IN = D_ATTN + 2 * D_KV + 2 * D_SGU
EPS = 1e-6

LANES = 128
MXU_DIM = 256
VMEM_LIMIT_BYTES = 56 * 1024 * 1024

FF_CHUNK = MXU_DIM
N_FF_CHUNKS = D_FF // FF_CHUNK
TM = 512
TQ = 128
TK = 512
KV_BLOCKS_PER_STEP = 16
LOG2_E = 1.4426950408889634
SHIFT_MARGIN = 1.0 + 2.0 ** -8
SAFE_SHIFT_LOG2 = 60.0
F32_SUBLANES = 8
BF16_SUBLANES = 16
V_ROWS = HEAD_DIM + BF16_SUBLANES


def _rms_norm(x, g):
    ms = jnp.mean(x * x, axis=-1, keepdims=True)
    return x * lax.rsqrt(ms + EPS) * g


def _swiglu_residual(x, g_ref, wgu_ref, wd_ref, hid_ref):
    xn = _rms_norm(x, g_ref[...]).astype(BF16)
    for c in range(N_FF_CHUNKS):
        h = jnp.dot(xn, wgu_ref[:, 2 * FF_CHUNK * c:2 * FF_CHUNK * (c + 1)],
                    preferred_element_type=F32)
        a = jax.nn.silu(h[:, :FF_CHUNK]) * h[:, FF_CHUNK:]
        hid_ref[:, FF_CHUNK * c:FF_CHUNK * (c + 1)] = a.astype(BF16)
    y = jnp.dot(hid_ref[...], wd_ref[...], preferred_element_type=F32)
    return x + 0.5 * y


def _head_mean_square(t, bmat):
    t2 = t * t
    hi = t2.astype(BF16)
    lo = (t2 - hi.astype(F32)).astype(BF16)
    return (jnp.dot(hi, bmat, preferred_element_type=F32)
            + jnp.dot(lo, bmat, preferred_element_type=F32))


def _rope_block(t, cos_t, sin_t, even_lane):
    nxt = pltpu.roll(t, LANES - 1, axis=1)
    prv = pltpu.roll(t, 1, axis=1)
    return t * cos_t + jnp.where(even_lane, nxt, prv) * sin_t


def _front_kernel(x_ref, g1_ref, wgu_ref, wd_ref, gmix_ref, win_ref, gq_ref, gk_ref,
                  bmat_ref, cos_ref, sin_ref, gsgu_ref, ws2_ref, bias_ref, gsout_ref,
                  x1_ref, qt_ref, k_ref, vt_ref, kn2_ref, sgu_ref, hid_ref, gate_ref):
    tm = x_ref.shape[0]
    x1 = _swiglu_residual(x_ref[...], g1_ref, wgu_ref, wd_ref, hid_ref)
    x1_ref[...] = x1

    h = _rms_norm(x1, gmix_ref[...]).astype(BF16)
    proj = jnp.dot(h, win_ref[...], preferred_element_type=F32)
    q = proj[:, :D_ATTN]
    k = proj[:, D_ATTN:D_ATTN + D_KV]
    v = proj[:, D_ATTN + D_KV:D_ATTN + 2 * D_KV]
    z = proj[:, D_ATTN + 2 * D_KV:]

    lane = lax.broadcasted_iota(jnp.int32, (tm, LANES), 1)
    even_lane = (lane % 2) == 0
    cos_t = cos_ref[...]
    sin_t = sin_ref[...]
    bmat = bmat_ref[...]

    qn = q * lax.rsqrt(_head_mean_square(q, bmat) + EPS) * gq_ref[...]
    scale = HEAD_DIM ** -0.5 * LOG2_E
    for j in range(D_ATTN // LANES):
        blk = _rope_block(qn[:, LANES * j:LANES * (j + 1)], cos_t, sin_t, even_lane)
        qt_ref[LANES * j:LANES * (j + 1), :] = (blk * scale).T.astype(BF16)

    bmat_kv = bmat[:D_KV, :D_KV]
    kn = k * lax.rsqrt(_head_mean_square(k, bmat_kv) + EPS) * gk_ref[...]
    k_bf = _rope_block(kn, cos_t, sin_t, even_lane).astype(BF16)
    k_ref[...] = k_bf
    k_norm2 = _head_mean_square(k_bf.astype(F32), bmat_kv) * HEAD_DIM
    kn2_ref[...] = jnp.broadcast_to(jnp.max(k_norm2, axis=0, keepdims=True), kn2_ref.shape)

    v_t = v.T.astype(BF16)
    ones = jnp.ones((V_ROWS - HEAD_DIM, tm), BF16)
    for kv in range(N_KV_HEADS):
        vt_ref[V_ROWS * kv:V_ROWS * kv + HEAD_DIM, :] = v_t[HEAD_DIM * kv:HEAD_DIM * (kv + 1), :]
        vt_ref[V_ROWS * kv + HEAD_DIM:V_ROWS * (kv + 1), :] = ones

    zz = jax.nn.gelu(z)
    u = zz[:, :D_SGU]
    vvn = _rms_norm(zz[:, D_SGU:], gsgu_ref[...]).astype(BF16)
    low_c = lax.broadcasted_iota(jnp.int32, (CHUNK, LANES), 1) < HEAD_DIM
    for c in range(tm // CHUNK):
        for j in range(D_SGU // LANES):
            blk = vvn[CHUNK * c:CHUNK * (c + 1), LANES * j:LANES * (j + 1)]
            r = jnp.dot(ws2_ref[j], blk, preferred_element_type=F32)
            f = jnp.where(low_c, r[:CHUNK], r[CHUNK:]) + bias_ref[:, LANES * j:LANES * (j + 1)]
            gate_ref[CHUNK * c:CHUNK * (c + 1), LANES * j:LANES * (j + 1)] = (
                u[CHUNK * c:CHUNK * (c + 1), LANES * j:LANES * (j + 1)] * f)
    sgu_ref[...] = _rms_norm(gate_ref[...], gsout_ref[...]).astype(BF16)


def _attn_kernel(qt_ref, k_ref, vt_ref, kn2_ref, gout_ref, o_ref, w_ref, acc_ref):
    tq = qt_ref.shape[1]
    gcols = KV_GROUP * tq
    n_kv_blocks = k_ref.shape[0] // TK

    key_max2 = jnp.max(kn2_ref[...], axis=0, keepdims=True)
    shifts = []
    for g in range(N_KV_HEADS):
        q_norm2 = []
        for hh in range(KV_GROUP):
            h = KV_GROUP * g + hh
            q_h = qt_ref[HEAD_DIM * h:HEAD_DIM * (h + 1), :]
            w_ref[g, HEAD_DIM * g:HEAD_DIM * (g + 1), tq * hh:tq * (hh + 1)] = q_h
            q_f = q_h.astype(F32)
            q_norm2.append(jnp.sum(q_f * q_f, axis=0, keepdims=True))
        w_ref[g, HEAD_DIM * (1 - g):HEAD_DIM * (2 - g), :] = jnp.zeros((HEAD_DIM, gcols), BF16)
        k_max2 = jnp.max(key_max2[:, HEAD_DIM * g:HEAD_DIM * (g + 1)], axis=1, keepdims=True)
        shifts.append(jnp.sqrt(jnp.concatenate(q_norm2, axis=1) * k_max2) * SHIFT_MARGIN)
    acc_ref[...] = jnp.zeros(acc_ref.shape, F32)
    shift_is_safe = jnp.maximum(jnp.max(shifts[0]), jnp.max(shifts[1])) <= SAFE_SHIFT_LOG2

    def scores(i, g):
        off = pl.multiple_of(i * TK, TK)
        s = jnp.dot(k_ref[pl.ds(off, TK), :], w_ref[g], preferred_element_type=F32)
        vt = vt_ref[V_ROWS * g:V_ROWS * (g + 1), pl.ds(off, TK)]
        return s, vt

    @pl.when(shift_is_safe)
    def _():
        def kv_step(i, carry):
            order = [(u, g) for u in range(KV_BLOCKS_PER_STEP) for g in range(N_KV_HEADS)]
            pv = [None] * N_KV_HEADS
            nxt = scores(i * KV_BLOCKS_PER_STEP + order[0][0], order[0][1])
            for n, (u, g) in enumerate(order):
                s, vt = nxt
                if n + 1 < len(order):
                    nxt = scores(i * KV_BLOCKS_PER_STEP + order[n + 1][0], order[n + 1][1])
                p = jnp.exp2(s - shifts[g]).astype(BF16)
                d = jnp.dot(vt, p, preferred_element_type=F32)
                pv[g] = d if pv[g] is None else pv[g] + d
            for g in range(N_KV_HEADS):
                acc_ref[g] += pv[g]
            return carry
        lax.fori_loop(0, n_kv_blocks // KV_BLOCKS_PER_STEP, kv_step, 0)

    @pl.when(jnp.logical_not(shift_is_safe))
    def _():
        def kv_step(i, ms):
            new_ms = []
            for g in range(N_KV_HEADS):
                s, vt = scores(i, g)
                m_new = jnp.maximum(ms[g], jnp.max(s, axis=0, keepdims=True))
                alpha = jnp.exp2(ms[g] - m_new)
                p = jnp.exp2(s - m_new).astype(BF16)
                acc_ref[g] = alpha * acc_ref[g] + jnp.dot(vt, p, preferred_element_type=F32)
                new_ms.append(m_new)
            return tuple(new_ms)
        m0 = jnp.full((1, gcols), -jnp.inf, F32)
        lax.fori_loop(0, n_kv_blocks, kv_step, (m0, m0))

    parts = []
    for g in range(N_KV_HEADS):
        a = acc_ref[g]
        o = a[:HEAD_DIM] / a[HEAD_DIM:HEAD_DIM + 1]
        parts.extend(o[:, tq * hh:tq * (hh + 1)] for hh in range(KV_GROUP))
    attn_t = jnp.concatenate(parts, axis=0)
    ms = jnp.mean(attn_t * attn_t, axis=0, keepdims=True)
    attn_n = (attn_t * lax.rsqrt(ms + EPS)).T * gout_ref[...]
    o_ref[...] = attn_n.astype(BF16)


def _back_kernel(x1_ref, attn_ref, sgu_ref, wo_a_ref, wo_s_ref, g2_ref, wgu_ref, wd_ref,
                 gfin_ref, o_ref, hid_ref):
    mix = (jnp.dot(attn_ref[...], wo_a_ref[...], preferred_element_type=F32)
           + jnp.dot(sgu_ref[...], wo_s_ref[...], preferred_element_type=F32))
    x2 = x1_ref[...] + mix
    x3 = _swiglu_residual(x2, g2_ref, wgu_ref, wd_ref, hid_ref)
    o_ref[...] = _rms_norm(x3, gfin_ref[...])


def _const_spec(shape):
    zeros = (0,) * len(shape)
    return pl.BlockSpec(shape, lambda *_: zeros, pipeline_mode=pl.Buffered(1))


def _interleave_gate_up(w_gate, w_up):
    d = w_gate.shape[0]
    g = w_gate.reshape(d, N_FF_CHUNKS, FF_CHUNK)
    u = w_up.reshape(d, N_FF_CHUNKS, FF_CHUNK)
    return jnp.concatenate([g, u], axis=-1).reshape(d, 2 * D_FF).astype(BF16)


def _rope_tables(seq):
    rows = seq // GRID_W
    row_idx = jnp.repeat(jnp.arange(rows, dtype=F32), GRID_W)
    col_idx = jnp.tile(jnp.arange(GRID_W, dtype=F32), rows)
    inv = 1.0 / (ROPE_THETA ** (jnp.arange(0, AXIS_DIM, 2, dtype=F32) / AXIS_DIM))
    ang = jnp.concatenate([row_idx[:, None] * inv, col_idx[:, None] * inv], axis=-1)
    cos = jnp.repeat(jnp.cos(ang), 2, axis=-1)
    sin = jnp.repeat(jnp.sin(ang), 2, axis=-1)
    sign = jnp.tile(jnp.array([-1.0, 1.0], F32), HEAD_DIM // 2)
    reps = LANES // HEAD_DIM
    return jnp.tile(cos, (1, reps)), jnp.tile(sin * sign, (1, reps))


def kernel(x, g_ffn1, w1_gate, w1_up, w1_down, g_mix, w_in, g_q, g_k, g_sgu, w_s, b_s,
           g_attn_out, g_sgu_out, w_out, g_ffn2, w2_gate, w2_up, w2_down, g_final):
    batch, seq, d_model = x.shape
    assert d_model == D_MODEL and seq % TM == 0 and seq % TK == 0 and seq % TQ == 0
    assert g_ffn1.shape[0] == 1, "single-layer block"
    n_tok = batch * seq
    tiles_per_seq = seq // TM

    row = lambda a: a.reshape(1, -1).astype(F32)
    wgu1 = _interleave_gate_up(w1_gate[0], w1_up[0])
    wgu2 = _interleave_gate_up(w2_gate[0], w2_up[0])
    wd1 = w1_down[0].astype(BF16)
    wd2 = w2_down[0].astype(BF16)
    win = w_in[0].astype(BF16)
    wo_a = w_out[0, :D_ATTN].astype(BF16)
    wo_s = w_out[0, D_ATTN:].astype(BF16)
    gq = jnp.tile(row(g_q[0]), (1, N_HEADS))
    gk = jnp.tile(row(g_k[0]), (1, N_KV_HEADS))
    head_id = jnp.arange(D_ATTN) // HEAD_DIM
    bmat = jnp.where(head_id[:, None] == head_id[None, :], 1.0 / HEAD_DIM, 0.0).astype(BF16)
    cos_t, sin_t = _rope_tables(seq)
    ws2 = w_s[0].reshape(N_SGU_GROUPS // 2, 2 * CHUNK, CHUNK).astype(BF16)
    bias = jnp.repeat(b_s[0].T, D_SGU // N_SGU_GROUPS, axis=1).astype(F32)

    params = pltpu.CompilerParams(dimension_semantics=("arbitrary",),
                                  vmem_limit_bytes=VMEM_LIMIT_BYTES)
    tok_spec = lambda width: pl.BlockSpec((TM, width), lambda i: (i, 0))
    pos_spec = pl.BlockSpec((TM, LANES), lambda i: (i % tiles_per_seq, 0))

    seq_major_spec = lambda rows: pl.BlockSpec(
        (None, rows, TM), lambda i: (i // tiles_per_seq, 0, i % tiles_per_seq))

    x1, qt, k, vt, kn2, sgu = pl.pallas_call(
        _front_kernel,
        grid=(n_tok // TM,),
        in_specs=[
            tok_spec(D_MODEL),
            _const_spec((1, D_MODEL)), _const_spec((D_MODEL, 2 * D_FF)), _const_spec((D_FF, D_MODEL)),
            _const_spec((1, D_MODEL)), _const_spec((D_MODEL, D_IN)),
            _const_spec((1, D_ATTN)), _const_spec((1, D_KV)), _const_spec((D_ATTN, D_ATTN)),
            pos_spec, pos_spec,
            _const_spec((1, D_SGU)), _const_spec((N_SGU_GROUPS // 2, 2 * CHUNK, CHUNK)),
            _const_spec((CHUNK, D_SGU)), _const_spec((1, D_SGU)),
        ],
        out_specs=[
            tok_spec(D_MODEL),
            seq_major_spec(D_ATTN),
            tok_spec(D_KV),
            seq_major_spec(N_KV_HEADS * V_ROWS),
            pl.BlockSpec((F32_SUBLANES, D_KV), lambda i: (i, 0)),
            tok_spec(D_SGU),
        ],
        out_shape=[
            jax.ShapeDtypeStruct((n_tok, D_MODEL), F32),
            jax.ShapeDtypeStruct((batch, D_ATTN, seq), BF16),
            jax.ShapeDtypeStruct((n_tok, D_KV), BF16),
            jax.ShapeDtypeStruct((batch, N_KV_HEADS * V_ROWS, seq), BF16),
            jax.ShapeDtypeStruct((n_tok // TM * F32_SUBLANES, D_KV), F32),
            jax.ShapeDtypeStruct((n_tok, D_SGU), BF16),
        ],
        scratch_shapes=[pltpu.VMEM((TM, D_FF), BF16), pltpu.VMEM((TM, D_SGU), F32)],
        compiler_params=params,
        name="front_ffn_proj",
    )(x.reshape(n_tok, D_MODEL), row(g_ffn1[0]), wgu1, wd1, row(g_mix[0]), win, gq, gk, bmat,
      cos_t, sin_t, row(g_sgu[0]), ws2, bias, row(g_sgu_out[0]))

    attn = pl.pallas_call(
        _attn_kernel,
        grid=(batch, seq // TQ),
        in_specs=[
            pl.BlockSpec((None, D_ATTN, TQ), lambda b, i: (b, 0, i)),
            pl.BlockSpec((None, seq, D_KV), lambda b, i: (b, 0, 0)),
            pl.BlockSpec((None, N_KV_HEADS * V_ROWS, seq), lambda b, i: (b, 0, 0)),
            pl.BlockSpec((None, tiles_per_seq * F32_SUBLANES, D_KV), lambda b, i: (b, 0, 0)),
            _const_spec((1, D_ATTN)),
        ],
        out_specs=pl.BlockSpec((None, TQ, D_ATTN), lambda b, i: (b, i, 0)),
        out_shape=jax.ShapeDtypeStruct((batch, seq, D_ATTN), BF16),
        scratch_shapes=[
            pltpu.VMEM((N_KV_HEADS, 2 * HEAD_DIM, KV_GROUP * TQ), BF16),
            pltpu.VMEM((N_KV_HEADS, V_ROWS, KV_GROUP * TQ), F32),
        ],
        compiler_params=pltpu.CompilerParams(dimension_semantics=("arbitrary", "arbitrary"),
                                             vmem_limit_bytes=VMEM_LIMIT_BYTES),
        name="gqa_attention",
    )(qt, k.reshape(batch, seq, D_KV), vt,
      kn2.reshape(batch, tiles_per_seq * F32_SUBLANES, D_KV), row(g_attn_out[0]))

    out = pl.pallas_call(
        _back_kernel,
        grid=(n_tok // TM,),
        in_specs=[
            tok_spec(D_MODEL), tok_spec(D_ATTN), tok_spec(D_SGU),
            _const_spec((D_ATTN, D_MODEL)), _const_spec((D_SGU, D_MODEL)),
            _const_spec((1, D_MODEL)), _const_spec((D_MODEL, 2 * D_FF)), _const_spec((D_FF, D_MODEL)),
            _const_spec((1, D_MODEL)),
        ],
        out_specs=tok_spec(D_MODEL),
        out_shape=jax.ShapeDtypeStruct((n_tok, D_MODEL), F32),
        scratch_shapes=[pltpu.VMEM((TM, D_FF), BF16)],
        compiler_params=params,
        name="back_proj_ffn",
    )(x1, attn.reshape(n_tok, D_ATTN), sgu, wo_a, wo_s, row(g_ffn2[0]), wgu2, wd2, row(g_final[0]))

    return out.reshape(batch, seq, D_MODEL)
```

```python
import functools

import jax
import jax.numpy as jnp
from jax import lax
from jax.experimental import pallas as pl
from jax.experimental.pallas import tpu as pltpu

F32 = jnp.float32
BF16 = jnp.bfloat16

D_MODEL = 1024
D_FF = 2816
N_HEADS = 8
HEAD_DIM = 64
N_KV_HEADS = 2
KV_GROUP = N_HEADS // N_KV_HEADS
D_ATTN = N_HEADS * HEAD_DIM
D_KV = N_KV_HEADS * HEAD_DIM
ROPE_THETA = 10000.0
AXIS_DIM = HEAD_DIM // 2
GRID_W = 64
N_SGU_GROUPS = 8
D_SGU = 512
CHUNK = 128
D_IN = D_ATTN + 2 * D_KV + 2 * D_SGU
EPS = 1e-6

LANES = 128
MXU_DIM = 256
VMEM_LIMIT_BYTES = 56 * 1024 * 1024

FF_CHUNK = MXU_DIM
N_FF_CHUNKS = D_FF // FF_CHUNK
TM = 512
TQ = 256
TK = 512
KV_BLOCKS_PER_STEP = 16
LOG2_E = 1.4426950408889634
SHIFT_MARGIN = 1.0 + 2.0 ** -8
SAFE_SHIFT_LOG2 = 60.0
F32_SUBLANES = 8
BF16_SUBLANES = 16
V_ROWS = HEAD_DIM + BF16_SUBLANES


def _rms_norm(x, g):
    ms = jnp.mean(x * x, axis=-1, keepdims=True)
    return x * lax.rsqrt(ms + EPS) * g


def _swiglu_residual(x, g_ref, wgu_ref, wd_ref, hid_ref):
    xn = _rms_norm(x, g_ref[...]).astype(BF16)
    for c in range(N_FF_CHUNKS):
        h = jnp.dot(xn, wgu_ref[:, 2 * FF_CHUNK * c:2 * FF_CHUNK * (c + 1)],
                    preferred_element_type=F32)
        a = jax.nn.silu(h[:, :FF_CHUNK]) * h[:, FF_CHUNK:]
        hid_ref[:, FF_CHUNK * c:FF_CHUNK * (c + 1)] = a.astype(BF16)
    y = jnp.dot(hid_ref[...], wd_ref[...], preferred_element_type=F32)
    return x + 0.5 * y


def _head_mean_square(t, bmat):
    width = bmat.shape[0]
    t2 = (t * t).astype(BF16)
    slabs = [jnp.dot(t2[:, c:c + width], bmat, preferred_element_type=F32)
             for c in range(0, t.shape[1], width)]
    return slabs[0] if len(slabs) == 1 else jnp.concatenate(slabs, axis=1)


def _rope_block(t, cos_t, sin_t, even_lane):
    nxt = pltpu.roll(t, LANES - 1, axis=1)
    prv = pltpu.roll(t, 1, axis=1)
    return t * cos_t + jnp.where(even_lane, nxt, prv) * sin_t


def _front_kernel(x_ref, g1_ref, wgu_ref, wd_ref, gmix_ref, win_ref, gq_ref, gk_ref,
                  bmat_ref, cos_ref, sin_ref, gsgu_ref, ws2_ref, bias_ref, gsout_ref,
                  x1_ref, qt_ref, k_ref, vt_ref, kn2_ref, sgu_ref, hid_ref, gate_ref):
    tm = x_ref.shape[0]
    x1 = _swiglu_residual(x_ref[...], g1_ref, wgu_ref, wd_ref, hid_ref)
    x1_ref[...] = x1

    h = _rms_norm(x1, gmix_ref[...]).astype(BF16)
    proj = jnp.dot(h, win_ref[...], preferred_element_type=F32)
    q = proj[:, :D_ATTN]
    k = proj[:, D_ATTN:D_ATTN + D_KV]
    v = proj[:, D_ATTN + D_KV:D_ATTN + 2 * D_KV]
    z = proj[:, D_ATTN + 2 * D_KV:]

    lane = lax.broadcasted_iota(jnp.int32, (tm, LANES), 1)
    even_lane = (lane % 2) == 0
    cos_t = cos_ref[...]
    sin_t = sin_ref[...]
    bmat = bmat_ref[...]

    qn = q * lax.rsqrt(_head_mean_square(q, bmat) + EPS) * gq_ref[...]
    scale = HEAD_DIM ** -0.5 * LOG2_E
    for j in range(D_ATTN // LANES):
        blk = _rope_block(qn[:, LANES * j:LANES * (j + 1)], cos_t, sin_t, even_lane)
        qt_ref[LANES * j:LANES * (j + 1), :] = (blk * scale).T.astype(BF16)

    bmat_kv = bmat[:D_KV, :D_KV]
    kn = k * lax.rsqrt(_head_mean_square(k, bmat_kv) + EPS) * gk_ref[...]
    k_bf = _rope_block(kn, cos_t, sin_t, even_lane).astype(BF16)
    k_ref[...] = k_bf
    k_norm2 = _head_mean_square(k_bf.astype(F32), bmat_kv) * HEAD_DIM
    kn2_ref[...] = jnp.broadcast_to(jnp.max(k_norm2, axis=0, keepdims=True), kn2_ref.shape)

    v_t = v.T.astype(BF16)
    ones = jnp.ones((V_ROWS - HEAD_DIM, tm), BF16)
    for kv in range(N_KV_HEADS):
        vt_ref[V_ROWS * kv:V_ROWS * kv + HEAD_DIM, :] = v_t[HEAD_DIM * kv:HEAD_DIM * (kv + 1), :]
        vt_ref[V_ROWS * kv + HEAD_DIM:V_ROWS * (kv + 1), :] = ones

    zz = jax.nn.gelu(z)
    u = zz[:, :D_SGU]
    vvn = _rms_norm(zz[:, D_SGU:], gsgu_ref[...]).astype(BF16)
    low_c = lax.broadcasted_iota(jnp.int32, (CHUNK, LANES), 1) < HEAD_DIM
    for c in range(0, tm // CHUNK, 2):
        for j in range(D_SGU // LANES):
            cols = slice(LANES * j, LANES * (j + 1))
            blk = jnp.concatenate([vvn[CHUNK * c:CHUNK * (c + 1), cols],
                                   vvn[CHUNK * (c + 1):CHUNK * (c + 2), cols]], axis=1)
            r = jnp.dot(ws2_ref[j], blk, preferred_element_type=F32)
            for e in range(2):
                rows = slice(CHUNK * (c + e), CHUNK * (c + e + 1))
                r_e = r[:, LANES * e:LANES * (e + 1)]
                f = jnp.where(low_c, r_e[:CHUNK], r_e[CHUNK:]) + bias_ref[:, cols]
                gate_ref[rows, cols] = u[rows, cols] * f
    sgu_ref[...] = _rms_norm(gate_ref[...], gsout_ref[...]).astype(BF16)


def _attn_kernel(qt_ref, k_ref, vt_ref, kn2_ref, gout_ref, o_ref, w_ref, acc_ref):
    tq = qt_ref.shape[1]
    gcols = KV_GROUP * tq
    n_kv_blocks = k_ref.shape[0] // TK

    key_max2 = jnp.max(kn2_ref[...], axis=0, keepdims=True)
    shifts = []
    for g in range(N_KV_HEADS):
        q_norm2 = []
        for hh in range(KV_GROUP):
            h = KV_GROUP * g + hh
            q_h = qt_ref[HEAD_DIM * h:HEAD_DIM * (h + 1), :]
            w_ref[g, HEAD_DIM * g:HEAD_DIM * (g + 1), tq * hh:tq * (hh + 1)] = q_h
            q_f = q_h.astype(F32)
            q_norm2.append(jnp.sum(q_f * q_f, axis=0, keepdims=True))
        w_ref[g, HEAD_DIM * (1 - g):HEAD_DIM * (2 - g), :] = jnp.zeros((HEAD_DIM, gcols), BF16)
        k_max2 = jnp.max(key_max2[:, HEAD_DIM * g:HEAD_DIM * (g + 1)], axis=1, keepdims=True)
        shifts.append(jnp.sqrt(jnp.concatenate(q_norm2, axis=1) * k_max2) * SHIFT_MARGIN)
    acc_ref[...] = jnp.zeros(acc_ref.shape, F32)
    shift_is_safe = jnp.maximum(jnp.max(shifts[0]), jnp.max(shifts[1])) <= SAFE_SHIFT_LOG2

    def scores(i, g):
        off = pl.multiple_of(i * TK, TK)
        s = jnp.dot(k_ref[pl.ds(off, TK), :], w_ref[g], preferred_element_type=F32)
        vt = vt_ref[V_ROWS * g:V_ROWS * (g + 1), pl.ds(off, TK)]
        return s, vt

    @pl.when(shift_is_safe)
    def _():
        def kv_step(i, carry):
            order = [(u, g) for u in range(KV_BLOCKS_PER_STEP) for g in range(N_KV_HEADS)]
            pv = [None] * N_KV_HEADS
            nxt = scores(i * KV_BLOCKS_PER_STEP + order[0][0], order[0][1])
            for n, (u, g) in enumerate(order):
                s, vt = nxt
                if n + 1 < len(order):
                    nxt = scores(i * KV_BLOCKS_PER_STEP + order[n + 1][0], order[n + 1][1])
                p = jnp.exp2(s - shifts[g]).astype(BF16)
                d = jnp.dot(vt, p, preferred_element_type=F32)
                pv[g] = d if pv[g] is None else pv[g] + d
            for g in range(N_KV_HEADS):
                acc_ref[g] += pv[g]
            return carry
        lax.fori_loop(0, n_kv_blocks // KV_BLOCKS_PER_STEP, kv_step, 0)

    @pl.when(jnp.logical_not(shift_is_safe))
    def _():
        def kv_step(i, ms):
            new_ms = []
            for g in range(N_KV_HEADS):
                s, vt = scores(i, g)
                m_new = jnp.maximum(ms[g], jnp.max(s, axis=0, keepdims=True))
                alpha = jnp.exp2(ms[g] - m_new)
                p = jnp.exp2(s - m_new).astype(BF16)
                acc_ref[g] = alpha * acc_ref[g] + jnp.dot(vt, p, preferred_element_type=F32)
                new_ms.append(m_new)
            return tuple(new_ms)
        m0 = jnp.full((1, gcols), -jnp.inf, F32)
        lax.fori_loop(0, n_kv_blocks, kv_step, (m0, m0))

    parts = []
    for g in range(N_KV_HEADS):
        a = acc_ref[g]
        o = a[:HEAD_DIM] / a[HEAD_DIM:HEAD_DIM + 1]
        parts.extend(o[:, tq * hh:tq * (hh + 1)] for hh in range(KV_GROUP))
    attn_t = jnp.concatenate(parts, axis=0)
    ms = jnp.mean(attn_t * attn_t, axis=0, keepdims=True)
    attn_n = (attn_t * lax.rsqrt(ms + EPS)).T * gout_ref[...]
    o_ref[...] = attn_n.astype(BF16)


def _back_kernel(x1_ref, attn_ref, sgu_ref, wo_a_ref, wo_s_ref, g2_ref, wgu_ref, wd_ref,
                 gfin_ref, o_ref, hid_ref):
    mix = (jnp.dot(attn_ref[...], wo_a_ref[...], preferred_element_type=F32)
           + jnp.dot(sgu_ref[...], wo_s_ref[...], preferred_element_type=F32))
    x2 = x1_ref[...] + mix
    x3 = _swiglu_residual(x2, g2_ref, wgu_ref, wd_ref, hid_ref)
    o_ref[...] = _rms_norm(x3, gfin_ref[...])


def _const_spec(shape):
    zeros = (0,) * len(shape)
    return pl.BlockSpec(shape, lambda *_: zeros, pipeline_mode=pl.Buffered(1))


def _interleave_gate_up(w_gate, w_up):
    d = w_gate.shape[0]
    g = w_gate.reshape(d, N_FF_CHUNKS, FF_CHUNK)
    u = w_up.reshape(d, N_FF_CHUNKS, FF_CHUNK)
    return jnp.concatenate([g, u], axis=-1).reshape(d, 2 * D_FF).astype(BF16)


def _rope_tables(seq):
    rows = seq // GRID_W
    row_idx = jnp.repeat(jnp.arange(rows, dtype=F32), GRID_W)
    col_idx = jnp.tile(jnp.arange(GRID_W, dtype=F32), rows)
    inv = 1.0 / (ROPE_THETA ** (jnp.arange(0, AXIS_DIM, 2, dtype=F32) / AXIS_DIM))
    ang = jnp.concatenate([row_idx[:, None] * inv, col_idx[:, None] * inv], axis=-1)
    cos = jnp.repeat(jnp.cos(ang), 2, axis=-1)
    sin = jnp.repeat(jnp.sin(ang), 2, axis=-1)
    sign = jnp.tile(jnp.array([-1.0, 1.0], F32), HEAD_DIM // 2)
    reps = LANES // HEAD_DIM
    return jnp.tile(cos, (1, reps)), jnp.tile(sin * sign, (1, reps))


def kernel(x, g_ffn1, w1_gate, w1_up, w1_down, g_mix, w_in, g_q, g_k, g_sgu, w_s, b_s,
           g_attn_out, g_sgu_out, w_out, g_ffn2, w2_gate, w2_up, w2_down, g_final):
    batch, seq, d_model = x.shape
    assert d_model == D_MODEL and seq % TM == 0 and seq % TK == 0 and seq % TQ == 0
    assert g_ffn1.shape[0] == 1, "single-layer block"
    n_tok = batch * seq
    tiles_per_seq = seq // TM

    row = lambda a: a.reshape(1, -1).astype(F32)
    wgu1 = _interleave_gate_up(w1_gate[0], w1_up[0])
    wgu2 = _interleave_gate_up(w2_gate[0], w2_up[0])
    wd1 = w1_down[0].astype(BF16)
    wd2 = w2_down[0].astype(BF16)
    win = w_in[0].astype(BF16)
    wo_a = w_out[0, :D_ATTN].astype(BF16)
    wo_s = w_out[0, D_ATTN:].astype(BF16)
    gq = jnp.tile(row(g_q[0]), (1, N_HEADS))
    gk = jnp.tile(row(g_k[0]), (1, N_KV_HEADS))
    head_id = jnp.arange(MXU_DIM) // HEAD_DIM
    bmat = jnp.where(head_id[:, None] == head_id[None, :], 1.0 / HEAD_DIM, 0.0).astype(BF16)
    cos_t, sin_t = _rope_tables(seq)
    ws2 = w_s[0].reshape(N_SGU_GROUPS // 2, 2 * CHUNK, CHUNK).astype(BF16)
    bias = jnp.repeat(b_s[0].T, D_SGU // N_SGU_GROUPS, axis=1).astype(F32)

    params = pltpu.CompilerParams(dimension_semantics=("arbitrary",),
                                  vmem_limit_bytes=VMEM_LIMIT_BYTES)
    n_tiles = n_tok // TM
    tok_spec = lambda width: pl.BlockSpec((TM, width), lambda i: (i, 0))
    pos_spec = pl.BlockSpec((TM, LANES), lambda i: (i % tiles_per_seq, 0))
    seq_major_spec = lambda rows: pl.BlockSpec(
        (None, rows, TM), lambda i: (i // tiles_per_seq, 0, i % tiles_per_seq))

    x1, qt, k, vt, kn2, sgu = pl.pallas_call(
        _front_kernel,
        grid=(n_tiles,),
        in_specs=[
            tok_spec(D_MODEL),
            _const_spec((1, D_MODEL)), _const_spec((D_MODEL, 2 * D_FF)), _const_spec((D_FF, D_MODEL)),
            _const_spec((1, D_MODEL)), _const_spec((D_MODEL, D_IN)),
            _const_spec((1, D_ATTN)), _const_spec((1, D_KV)), _const_spec((MXU_DIM, MXU_DIM)),
            pos_spec, pos_spec,
            _const_spec((1, D_SGU)), _const_spec((N_SGU_GROUPS // 2, 2 * CHUNK, CHUNK)),
            _const_spec((CHUNK, D_SGU)), _const_spec((1, D_SGU)),
        ],
        out_specs=[
            tok_spec(D_MODEL),
            seq_major_spec(D_ATTN),
            tok_spec(D_KV),
            seq_major_spec(N_KV_HEADS * V_ROWS),
            pl.BlockSpec((F32_SUBLANES, D_KV), lambda i: (i, 0)),
            tok_spec(D_SGU),
        ],
        out_shape=[
            jax.ShapeDtypeStruct((n_tok, D_MODEL), F32),
            jax.ShapeDtypeStruct((batch, D_ATTN, seq), BF16),
            jax.ShapeDtypeStruct((n_tok, D_KV), BF16),
            jax.ShapeDtypeStruct((batch, N_KV_HEADS * V_ROWS, seq), BF16),
            jax.ShapeDtypeStruct((n_tok // TM * F32_SUBLANES, D_KV), F32),
            jax.ShapeDtypeStruct((n_tok, D_SGU), BF16),
        ],
        scratch_shapes=[pltpu.VMEM((TM, D_FF), BF16), pltpu.VMEM((TM, D_SGU), F32)],
        compiler_params=params,
        name="front_ffn_proj",
    )(x.reshape(n_tok, D_MODEL), row(g_ffn1[0]), wgu1, wd1, row(g_mix[0]), win, gq, gk, bmat,
      cos_t, sin_t, row(g_sgu[0]), ws2, bias, row(g_sgu_out[0]))

    attn = pl.pallas_call(
        _attn_kernel,
        grid=(batch, seq // TQ),
        in_specs=[
            pl.BlockSpec((None, D_ATTN, TQ), lambda b, i: (b, 0, i)),
            pl.BlockSpec((None, seq, D_KV), lambda b, i: (b, 0, 0)),
            pl.BlockSpec((None, N_KV_HEADS * V_ROWS, seq), lambda b, i: (b, 0, 0)),
            pl.BlockSpec((None, tiles_per_seq * F32_SUBLANES, D_KV), lambda b, i: (b, 0, 0)),
            _const_spec((1, D_ATTN)),
        ],
        out_specs=pl.BlockSpec((None, TQ, D_ATTN), lambda b, i: (b, i, 0)),
        out_shape=jax.ShapeDtypeStruct((batch, seq, D_ATTN), BF16),
        scratch_shapes=[
            pltpu.VMEM((N_KV_HEADS, 2 * HEAD_DIM, KV_GROUP * TQ), BF16),
            pltpu.VMEM((N_KV_HEADS, V_ROWS, KV_GROUP * TQ), F32),
        ],
        compiler_params=pltpu.CompilerParams(dimension_semantics=("arbitrary", "arbitrary"),
                                             vmem_limit_bytes=VMEM_LIMIT_BYTES),
        name="gqa_attention",
    )(qt, k.reshape(batch, seq, D_KV), vt,
      kn2.reshape(batch, tiles_per_seq * F32_SUBLANES, D_KV), row(g_attn_out[0]))

    out = pl.pallas_call(
        _back_kernel,
        grid=(n_tok // TM,),
        in_specs=[
            tok_spec(D_MODEL), tok_spec(D_ATTN), tok_spec(D_SGU),
            _const_spec((D_ATTN, D_MODEL)), _const_spec((D_SGU, D_MODEL)),
            _const_spec((1, D_MODEL)), _const_spec((D_MODEL, 2 * D_FF)), _const_spec((D_FF, D_MODEL)),
            _const_spec((1, D_MODEL)),
        ],
        out_specs=tok_spec(D_MODEL),
        out_shape=jax.ShapeDtypeStruct((n_tok, D_MODEL), F32),
        scratch_shapes=[pltpu.VMEM((TM, D_FF), BF16)],
        compiler_params=params,
        name="back_proj_ffn",
    )(x1, attn.reshape(n_tok, D_ATTN), sgu, wo_a, wo_s, row(g_ffn2[0]), wgu2, wd2, row(g_final[0]))

    return out.reshape(batch, seq, D_MODEL)
```

```python
import functools

import jax
import jax.numpy as jnp
from jax import lax
from jax.experimental import pallas as pl
from jax.experimental.pallas import tpu as pltpu

F32 = jnp.float32
BF16 = jnp.bfloat16

D_MODEL = 1024
D_FF = 2816
N_HEADS = 8
HEAD_DIM = 64
N_KV_HEADS = 2
KV_GROUP = N_HEADS // N_KV_HEADS
D_ATTN = N_HEADS * HEAD_DIM
D_KV = N_KV_HEADS * HEAD_DIM
ROPE_THETA = 10000.0
AXIS_DIM = HEAD_DIM // 2
GRID_W = 64
N_SGU_GROUPS = 8
D_SGU = 512
CHUNK = 128
D_IN = D_ATTN + 2 * D_KV + 2 * D_SGU
EPS = 1e-6

LANES = 128
MXU_DIM = 256
VMEM_LIMIT_BYTES = 56 * 1024 * 1024

FF_CHUNK = MXU_DIM
N_FF_CHUNKS = D_FF // FF_CHUNK
TM = 512
TQ = 512
TK = 256
COL_CHUNK = 2 * MXU_DIM
QK_LOOKAHEAD = 2
KV_BLOCKS_PER_STEP = 32
LOG2_E = 1.4426950408889634
SHIFT_MARGIN = 1.0 + 2.0 ** -8
SAFE_SHIFT_LOG2 = 60.0
F32_SUBLANES = 8
BF16_SUBLANES = 16
V_ROWS = HEAD_DIM + BF16_SUBLANES


def _rms_norm(x, g):
    ms = jnp.mean(x * x, axis=-1, keepdims=True)
    return x * lax.rsqrt(ms + EPS) * g


def _swiglu_residual(x, g_ref, wgu_ref, wd_ref, hid_ref):
    xn = _rms_norm(x, g_ref[...]).astype(BF16)
    for c in range(N_FF_CHUNKS):
        h = jnp.dot(xn, wgu_ref[:, 2 * FF_CHUNK * c:2 * FF_CHUNK * (c + 1)],
                    preferred_element_type=F32)
        a = jax.nn.silu(h[:, :FF_CHUNK]) * h[:, FF_CHUNK:]
        hid_ref[:, FF_CHUNK * c:FF_CHUNK * (c + 1)] = a.astype(BF16)
    y = jnp.dot(hid_ref[...], wd_ref[...], preferred_element_type=F32)
    return x + 0.5 * y


def _head_mean_square(t, bmat):
    width = bmat.shape[0]
    t2 = (t * t).astype(BF16)
    slabs = [jnp.dot(t2[:, c:c + width], bmat, preferred_element_type=F32)
             for c in range(0, t.shape[1], width)]
    return slabs[0] if len(slabs) == 1 else jnp.concatenate(slabs, axis=1)


def _rope_block(t, cos_t, sin_t, even_lane):
    nxt = pltpu.roll(t, LANES - 1, axis=1)
    prv = pltpu.roll(t, 1, axis=1)
    return t * cos_t + jnp.where(even_lane, nxt, prv) * sin_t


def _front_kernel(x_ref, g1_ref, wgu_ref, wd_ref, gmix_ref, win_ref, gq_ref, gk_ref,
                  bmat_ref, cos_ref, sin_ref, gsgu_ref, ws2_ref, bias_ref, gsout_ref,
                  x1_ref, qt_ref, k_ref, vt_ref, kn2_ref, sgu_ref, hid_ref, gate_ref):
    tm = x_ref.shape[0]
    x1 = _swiglu_residual(x_ref[...], g1_ref, wgu_ref, wd_ref, hid_ref)
    x1_ref[...] = x1

    h = _rms_norm(x1, gmix_ref[...]).astype(BF16)
    proj = jnp.dot(h, win_ref[...], preferred_element_type=F32)
    q = proj[:, :D_ATTN]
    k = proj[:, D_ATTN:D_ATTN + D_KV]
    v = proj[:, D_ATTN + D_KV:D_ATTN + 2 * D_KV]
    z = proj[:, D_ATTN + 2 * D_KV:]

    lane = lax.broadcasted_iota(jnp.int32, (tm, LANES), 1)
    even_lane = (lane % 2) == 0
    cos_t = cos_ref[...]
    sin_t = sin_ref[...]
    bmat = bmat_ref[...]

    qn = q * lax.rsqrt(_head_mean_square(q, bmat) + EPS) * gq_ref[...]
    scale = HEAD_DIM ** -0.5 * LOG2_E
    for j in range(D_ATTN // LANES):
        blk = _rope_block(qn[:, LANES * j:LANES * (j + 1)], cos_t, sin_t, even_lane)
        qt_ref[LANES * j:LANES * (j + 1), :] = (blk * scale).T.astype(BF16)

    bmat_kv = bmat[:D_KV, :D_KV]
    kn = k * lax.rsqrt(_head_mean_square(k, bmat_kv) + EPS) * gk_ref[...]
    k_bf = _rope_block(kn, cos_t, sin_t, even_lane).astype(BF16)
    k_ref[...] = k_bf
    k_norm2 = _head_mean_square(k_bf.astype(F32), bmat_kv) * HEAD_DIM
    kn2_ref[...] = jnp.broadcast_to(jnp.max(k_norm2, axis=0, keepdims=True), kn2_ref.shape)

    v_t = v.T.astype(BF16)
    ones = jnp.ones((V_ROWS - HEAD_DIM, tm), BF16)
    for kv in range(N_KV_HEADS):
        vt_ref[V_ROWS * kv:V_ROWS * kv + HEAD_DIM, :] = v_t[HEAD_DIM * kv:HEAD_DIM * (kv + 1), :]
        vt_ref[V_ROWS * kv + HEAD_DIM:V_ROWS * (kv + 1), :] = ones

    zz = jax.nn.gelu(z)
    u = zz[:, :D_SGU]
    vvn = _rms_norm(zz[:, D_SGU:], gsgu_ref[...]).astype(BF16)
    low_c = lax.broadcasted_iota(jnp.int32, (CHUNK, LANES), 1) < HEAD_DIM
    for c in range(0, tm // CHUNK, 2):
        for j in range(D_SGU // LANES):
            cols = slice(LANES * j, LANES * (j + 1))
            blk = jnp.concatenate([vvn[CHUNK * c:CHUNK * (c + 1), cols],
                                   vvn[CHUNK * (c + 1):CHUNK * (c + 2), cols]], axis=1)
            r = jnp.dot(ws2_ref[j], blk, preferred_element_type=F32)
            for e in range(2):
                rows = slice(CHUNK * (c + e), CHUNK * (c + e + 1))
                r_e = r[:, LANES * e:LANES * (e + 1)]
                f = jnp.where(low_c, r_e[:CHUNK], r_e[CHUNK:]) + bias_ref[:, cols]
                gate_ref[rows, cols] = u[rows, cols] * f
    sgu_ref[...] = _rms_norm(gate_ref[...], gsout_ref[...]).astype(BF16)


def _attn_kernel(qt_ref, k_ref, vt_ref, kn2_ref, gout_ref, o_ref, w_ref, acc_ref):
    tq = qt_ref.shape[1]
    gcols = KV_GROUP * tq
    n_kv_blocks = k_ref.shape[0] // TK

    key_max2 = jnp.max(kn2_ref[...], axis=0, keepdims=True)
    shifts = []
    for g in range(N_KV_HEADS):
        q_norm2 = []
        for hh in range(KV_GROUP):
            h = KV_GROUP * g + hh
            q_h = qt_ref[HEAD_DIM * h:HEAD_DIM * (h + 1), :]
            w_ref[g, HEAD_DIM * g:HEAD_DIM * (g + 1), tq * hh:tq * (hh + 1)] = q_h
            q_f = q_h.astype(F32)
            q_norm2.append(jnp.sum(q_f * q_f, axis=0, keepdims=True))
        w_ref[g, HEAD_DIM * (1 - g):HEAD_DIM * (2 - g), :] = jnp.zeros((HEAD_DIM, gcols), BF16)
        k_max2 = jnp.max(key_max2[:, HEAD_DIM * g:HEAD_DIM * (g + 1)], axis=1, keepdims=True)
        shifts.append(jnp.sqrt(jnp.concatenate(q_norm2, axis=1) * k_max2) * SHIFT_MARGIN)
    acc_ref[...] = jnp.zeros(acc_ref.shape, F32)
    shift_is_safe = jnp.maximum(jnp.max(shifts[0]), jnp.max(shifts[1])) <= SAFE_SHIFT_LOG2

    def scores(i, g):
        off = pl.multiple_of(i * TK, TK)
        s = jnp.dot(k_ref[pl.ds(off, TK), :], w_ref[g], preferred_element_type=F32)
        vt = vt_ref[V_ROWS * g:V_ROWS * (g + 1), pl.ds(off, TK)]
        return s, vt

    @pl.when(shift_is_safe)
    def _():
        units = [(g, c, u) for g in range(N_KV_HEADS) for c in range(gcols // COL_CHUNK)
                 for u in range(KV_BLOCKS_PER_STEP)]

        def kv_step(i, carry):
            def key_rows(u):
                return pl.ds(pl.multiple_of((i * KV_BLOCKS_PER_STEP + u) * TK, TK), TK)

            def unit_scores(n):
                g, c, u = units[n]
                return jnp.dot(k_ref[key_rows(u), :], w_ref[g, :, COL_CHUNK * c:COL_CHUNK * (c + 1)],
                               preferred_element_type=F32)

            pending = [unit_scores(n) for n in range(QK_LOOKAHEAD)]
            pv = None
            for n, (g, c, u) in enumerate(units):
                s = pending.pop(0)
                if n + QK_LOOKAHEAD < len(units):
                    pending.append(unit_scores(n + QK_LOOKAHEAD))
                cols = slice(COL_CHUNK * c, COL_CHUNK * (c + 1))
                p = jnp.exp2(s - shifts[g][:, cols]).astype(BF16)
                vt = vt_ref[V_ROWS * g:V_ROWS * (g + 1), key_rows(u)]
                d = jnp.dot(vt, p, preferred_element_type=F32)
                pv = d if u == 0 else pv + d
                if u == KV_BLOCKS_PER_STEP - 1:
                    acc_ref[g, :, cols] += pv
            return carry
        lax.fori_loop(0, n_kv_blocks // KV_BLOCKS_PER_STEP, kv_step, 0)

    @pl.when(jnp.logical_not(shift_is_safe))
    def _():
        def kv_step(i, ms):
            new_ms = []
            for g in range(N_KV_HEADS):
                s, vt = scores(i, g)
                m_new = jnp.maximum(ms[g], jnp.max(s, axis=0, keepdims=True))
                alpha = jnp.exp2(ms[g] - m_new)
                p = jnp.exp2(s - m_new).astype(BF16)
                acc_ref[g] = alpha * acc_ref[g] + jnp.dot(vt, p, preferred_element_type=F32)
                new_ms.append(m_new)
            return tuple(new_ms)
        m0 = jnp.full((1, gcols), -jnp.inf, F32)
        lax.fori_loop(0, n_kv_blocks, kv_step, (m0, m0))

    parts = []
    for g in range(N_KV_HEADS):
        a = acc_ref[g]
        o = a[:HEAD_DIM] / a[HEAD_DIM:HEAD_DIM + 1]
        parts.extend(o[:, tq * hh:tq * (hh + 1)] for hh in range(KV_GROUP))
    attn_t = jnp.concatenate(parts, axis=0)
    ms = jnp.mean(attn_t * attn_t, axis=0, keepdims=True)
    attn_n = (attn_t * lax.rsqrt(ms + EPS)).T * gout_ref[...]
    o_ref[...] = attn_n.astype(BF16)


def _back_kernel(x1_ref, attn_ref, sgu_ref, wo_a_ref, wo_s_ref, g2_ref, wgu_ref, wd_ref,
                 gfin_ref, o_ref, hid_ref):
    mix = (jnp.dot(attn_ref[...], wo_a_ref[...], preferred_element_type=F32)
           + jnp.dot(sgu_ref[...], wo_s_ref[...], preferred_element_type=F32))
    x2 = x1_ref[...] + mix
    x3 = _swiglu_residual(x2, g2_ref, wgu_ref, wd_ref, hid_ref)
    o_ref[...] = _rms_norm(x3, gfin_ref[...])


def _const_spec(shape):
    zeros = (0,) * len(shape)
    return pl.BlockSpec(shape, lambda *_: zeros, pipeline_mode=pl.Buffered(1))


def _interleave_gate_up(w_gate, w_up):
    d = w_gate.shape[0]
    g = w_gate.reshape(d, N_FF_CHUNKS, FF_CHUNK)
    u = w_up.reshape(d, N_FF_CHUNKS, FF_CHUNK)
    return jnp.concatenate([g, u], axis=-1).reshape(d, 2 * D_FF).astype(BF16)


def _rope_tables(seq):
    rows = seq // GRID_W
    row_idx = jnp.repeat(jnp.arange(rows, dtype=F32), GRID_W)
    col_idx = jnp.tile(jnp.arange(GRID_W, dtype=F32), rows)
    inv = 1.0 / (ROPE_THETA ** (jnp.arange(0, AXIS_DIM, 2, dtype=F32) / AXIS_DIM))
    ang = jnp.concatenate([row_idx[:, None] * inv, col_idx[:, None] * inv], axis=-1)
    cos = jnp.repeat(jnp.cos(ang), 2, axis=-1)
    sin = jnp.repeat(jnp.sin(ang), 2, axis=-1)
    sign = jnp.tile(jnp.array([-1.0, 1.0], F32), HEAD_DIM // 2)
    reps = LANES // HEAD_DIM
    return jnp.tile(cos, (1, reps)), jnp.tile(sin * sign, (1, reps))


def kernel(x, g_ffn1, w1_gate, w1_up, w1_down, g_mix, w_in, g_q, g_k, g_sgu, w_s, b_s,
           g_attn_out, g_sgu_out, w_out, g_ffn2, w2_gate, w2_up, w2_down, g_final):
    batch, seq, d_model = x.shape
    assert d_model == D_MODEL and seq % TM == 0 and seq % TK == 0 and seq % TQ == 0
    assert g_ffn1.shape[0] == 1, "single-layer block"
    n_tok = batch * seq
    tiles_per_seq = seq // TM

    row = lambda a: a.reshape(1, -1).astype(F32)
    wgu1 = _interleave_gate_up(w1_gate[0], w1_up[0])
    wgu2 = _interleave_gate_up(w2_gate[0], w2_up[0])
    wd1 = w1_down[0].astype(BF16)
    wd2 = w2_down[0].astype(BF16)
    win = w_in[0].astype(BF16)
    wo_a = w_out[0, :D_ATTN].astype(BF16)
    wo_s = w_out[0, D_ATTN:].astype(BF16)
    gq = jnp.tile(row(g_q[0]), (1, N_HEADS))
    gk = jnp.tile(row(g_k[0]), (1, N_KV_HEADS))
    head_id = jnp.arange(MXU_DIM) // HEAD_DIM
    bmat = jnp.where(head_id[:, None] == head_id[None, :], 1.0 / HEAD_DIM, 0.0).astype(BF16)
    cos_t, sin_t = _rope_tables(seq)
    ws2 = w_s[0].reshape(N_SGU_GROUPS // 2, 2 * CHUNK, CHUNK).astype(BF16)
    bias = jnp.repeat(b_s[0].T, D_SGU // N_SGU_GROUPS, axis=1).astype(F32)

    params = pltpu.CompilerParams(dimension_semantics=("arbitrary",),
                                  vmem_limit_bytes=VMEM_LIMIT_BYTES)
    n_tiles = n_tok // TM
    tok_spec = lambda width: pl.BlockSpec((TM, width), lambda i: (i, 0))
    pos_spec = pl.BlockSpec((TM, LANES), lambda i: (i % tiles_per_seq, 0))
    seq_major_spec = lambda rows: pl.BlockSpec(
        (None, rows, TM), lambda i: (i // tiles_per_seq, 0, i % tiles_per_seq))

    x1, qt, k, vt, kn2, sgu = pl.pallas_call(
        _front_kernel,
        grid=(n_tiles,),
        in_specs=[
            tok_spec(D_MODEL),
            _const_spec((1, D_MODEL)), _const_spec((D_MODEL, 2 * D_FF)), _const_spec((D_FF, D_MODEL)),
            _const_spec((1, D_MODEL)), _const_spec((D_MODEL, D_IN)),
            _const_spec((1, D_ATTN)), _const_spec((1, D_KV)), _const_spec((MXU_DIM, MXU_DIM)),
            pos_spec, pos_spec,
            _const_spec((1, D_SGU)), _const_spec((N_SGU_GROUPS // 2, 2 * CHUNK, CHUNK)),
            _const_spec((CHUNK, D_SGU)), _const_spec((1, D_SGU)),
        ],
        out_specs=[
            tok_spec(D_MODEL),
            seq_major_spec(D_ATTN),
            tok_spec(D_KV),
            seq_major_spec(N_KV_HEADS * V_ROWS),
            pl.BlockSpec((F32_SUBLANES, D_KV), lambda i: (i, 0)),
            tok_spec(D_SGU),
        ],
        out_shape=[
            jax.ShapeDtypeStruct((n_tok, D_MODEL), F32),
            jax.ShapeDtypeStruct((batch, D_ATTN, seq), BF16),
            jax.ShapeDtypeStruct((n_tok, D_KV), BF16),
            jax.ShapeDtypeStruct((batch, N_KV_HEADS * V_ROWS, seq), BF16),
            jax.ShapeDtypeStruct((n_tok // TM * F32_SUBLANES, D_KV), F32),
            jax.ShapeDtypeStruct((n_tok, D_SGU), BF16),
        ],
        scratch_shapes=[pltpu.VMEM((TM, D_FF), BF16), pltpu.VMEM((TM, D_SGU), F32)],
        compiler_params=params,
        name="front_ffn_proj",
    )(x.reshape(n_tok, D_MODEL), row(g_ffn1[0]), wgu1, wd1, row(g_mix[0]), win, gq, gk, bmat,
      cos_t, sin_t, row(g_sgu[0]), ws2, bias, row(g_sgu_out[0]))

    attn = pl.pallas_call(
        _attn_kernel,
        grid=(batch, seq // TQ),
        in_specs=[
            pl.BlockSpec((None, D_ATTN, TQ), lambda b, i: (b, 0, i)),
            pl.BlockSpec((None, seq, D_KV), lambda b, i: (b, 0, 0)),
            pl.BlockSpec((None, N_KV_HEADS * V_ROWS, seq), lambda b, i: (b, 0, 0)),
            pl.BlockSpec((None, tiles_per_seq * F32_SUBLANES, D_KV), lambda b, i: (b, 0, 0)),
            _const_spec((1, D_ATTN)),
        ],
        out_specs=pl.BlockSpec((None, TQ, D_ATTN), lambda b, i: (b, i, 0)),
        out_shape=jax.ShapeDtypeStruct((batch, seq, D_ATTN), BF16),
        scratch_shapes=[
            pltpu.VMEM((N_KV_HEADS, 2 * HEAD_DIM, KV_GROUP * TQ), BF16),
            pltpu.VMEM((N_KV_HEADS, V_ROWS, KV_GROUP * TQ), F32),
        ],
        compiler_params=pltpu.CompilerParams(dimension_semantics=("arbitrary", "arbitrary"),
                                             vmem_limit_bytes=VMEM_LIMIT_BYTES),
        name="gqa_attention",
    )(qt, k.reshape(batch, seq, D_KV), vt,
      kn2.reshape(batch, tiles_per_seq * F32_SUBLANES, D_KV), row(g_attn_out[0]))

    out = pl.pallas_call(
        _back_kernel,
        grid=(n_tok // TM,),
        in_specs=[
            tok_spec(D_MODEL), tok_spec(D_ATTN), tok_spec(D_SGU),
            _const_spec((D_ATTN, D_MODEL)), _const_spec((D_SGU, D_MODEL)),
            _const_spec((1, D_MODEL)), _const_spec((D_MODEL, 2 * D_FF)), _const_spec((D_FF, D_MODEL)),
            _const_spec((1, D_MODEL)),
        ],
        out_specs=tok_spec(D_MODEL),
        out_shape=jax.ShapeDtypeStruct((n_tok, D_MODEL), F32),
        scratch_shapes=[pltpu.VMEM((TM, D_FF), BF16)],
        compiler_params=params,
        name="back_proj_ffn",
    )(x1, attn.reshape(n_tok, D_ATTN), sgu, wo_a, wo_s, row(g_ffn2[0]), wgu2, wd2, row(g_final[0]))

    return out.reshape(batch, seq, D_MODEL)
```

```python
import functools

import jax
import jax.numpy as jnp
from jax import lax
from jax.experimental import pallas as pl
from jax.experimental.pallas import tpu as pltpu

F32 = jnp.float32
BF16 = jnp.bfloat16

D_MODEL = 1024
D_FF = 2816
N_HEADS = 8
HEAD_DIM = 64
N_KV_HEADS = 2
KV_GROUP = N_HEADS // N_KV_HEADS
D_ATTN = N_HEADS * HEAD_DIM
D_KV = N_KV_HEADS * HEAD_DIM
ROPE_THETA = 10000.0
AXIS_DIM = HEAD_DIM // 2
GRID_W = 64
N_SGU_GROUPS = 8
D_SGU = 512
CHUNK = 128
D_IN = D_ATTN + 2 * D_KV + 2 * D_SGU
EPS = 1e-6

LANES = 128
MXU_DIM = 256
VMEM_LIMIT_BYTES = 56 * 1024 * 1024

FF_CHUNK = MXU_DIM
N_FF_CHUNKS = D_FF // FF_CHUNK
TM = 512
TQ = 512
TK = 256
COL_CHUNK = 2 * MXU_DIM
QK_LOOKAHEAD = 2
KV_BLOCKS_PER_STEP = 32
LOG2_E = 1.4426950408889634
SHIFT_MARGIN = 1.0 + 2.0 ** -8
SAFE_SHIFT_LOG2 = 60.0
F32_SUBLANES = 8
BF16_SUBLANES = 16
V_ROWS = HEAD_DIM + BF16_SUBLANES


def _rms_norm(x, g):
    ms = jnp.mean(x * x, axis=-1, keepdims=True)
    return x * lax.rsqrt(ms + EPS) * g


def _swiglu_hidden(x, g_ref, wg_ref, wu_ref, hid_ref):
    xn = _rms_norm(x, g_ref[...]).astype(BF16)
    for c in range(N_FF_CHUNKS):
        cols = slice(FF_CHUNK * c, FF_CHUNK * (c + 1))
        gate = jnp.dot(xn, wg_ref[:, cols], preferred_element_type=F32)
        up = jnp.dot(xn, wu_ref[:, cols], preferred_element_type=F32)
        hid_ref[:, cols] = (jax.nn.silu(gate) * up).astype(BF16)


def _swiglu_down(x, wd_ref, hid_ref):
    return x + 0.5 * jnp.dot(hid_ref[...], wd_ref[...], preferred_element_type=F32)


def _swiglu_residual(x, g_ref, wg_ref, wu_ref, wd_ref, hid_ref):
    _swiglu_hidden(x, g_ref, wg_ref, wu_ref, hid_ref)
    return _swiglu_down(x, wd_ref, hid_ref)


def _head_mean_square(t, bmat):
    width = bmat.shape[0]
    t2 = (t * t).astype(BF16)
    slabs = [jnp.dot(t2[:, c:c + width], bmat, preferred_element_type=F32)
             for c in range(0, t.shape[1], width)]
    return slabs[0] if len(slabs) == 1 else jnp.concatenate(slabs, axis=1)


def _rope_block(t, cos_t, sin_t, even_lane):
    nxt = pltpu.roll(t, LANES - 1, axis=1)
    prv = pltpu.roll(t, 1, axis=1)
    return t * cos_t + jnp.where(even_lane, nxt, prv) * sin_t


def _front_kernel(x_ref, g1_ref, wg_ref, wu_ref, wd_ref, gmix_ref, win_ref, gq_ref, gk_ref,
                  bmat_ref, cos_ref, sin_ref, gsgu_ref, ws2_ref, bias_ref, gsout_ref,
                  x1_ref, qt_ref, k_ref, vt_ref, kn2_ref, sgu_ref, hid_ref, gate_ref, x1_scr):
    tm = x_ref.shape[0]

    @pl.when(pl.program_id(0) == 0)
    def _():
        x1_scr[...] = jnp.zeros(x1_scr.shape, F32)

    h = _rms_norm(x1_scr[...], gmix_ref[...]).astype(BF16)
    proj = jnp.dot(h, win_ref[...], preferred_element_type=F32)
    x = x_ref[...]
    _swiglu_hidden(x, g1_ref, wg_ref, wu_ref, hid_ref)
    q = proj[:, :D_ATTN]
    k = proj[:, D_ATTN:D_ATTN + D_KV]
    v = proj[:, D_ATTN + D_KV:D_ATTN + 2 * D_KV]
    z = proj[:, D_ATTN + 2 * D_KV:]

    lane = lax.broadcasted_iota(jnp.int32, (tm, LANES), 1)
    even_lane = (lane % 2) == 0
    cos_t = cos_ref[...]
    sin_t = sin_ref[...]
    bmat = bmat_ref[...]

    qn = q * lax.rsqrt(_head_mean_square(q, bmat) + EPS) * gq_ref[...]
    scale = HEAD_DIM ** -0.5 * LOG2_E
    for j in range(D_ATTN // LANES):
        blk = _rope_block(qn[:, LANES * j:LANES * (j + 1)], cos_t, sin_t, even_lane)
        qt_ref[LANES * j:LANES * (j + 1), :] = (blk * scale).T.astype(BF16)

    bmat_kv = bmat[:D_KV, :D_KV]
    kn = k * lax.rsqrt(_head_mean_square(k, bmat_kv) + EPS) * gk_ref[...]
    k_bf = _rope_block(kn, cos_t, sin_t, even_lane).astype(BF16)
    k_ref[...] = k_bf
    k_norm2 = _head_mean_square(k_bf.astype(F32), bmat_kv) * HEAD_DIM
    kn2_ref[...] = jnp.broadcast_to(jnp.max(k_norm2, axis=0, keepdims=True), kn2_ref.shape)

    v_t = v.T.astype(BF16)
    ones = jnp.ones((V_ROWS - HEAD_DIM, tm), BF16)
    for kv in range(N_KV_HEADS):
        vt_ref[V_ROWS * kv:V_ROWS * kv + HEAD_DIM, :] = v_t[HEAD_DIM * kv:HEAD_DIM * (kv + 1), :]
        vt_ref[V_ROWS * kv + HEAD_DIM:V_ROWS * (kv + 1), :] = ones

    zz = jax.nn.gelu(z)
    u = zz[:, :D_SGU]
    vvn = _rms_norm(zz[:, D_SGU:], gsgu_ref[...]).astype(BF16)
    low_c = lax.broadcasted_iota(jnp.int32, (CHUNK, LANES), 1) < HEAD_DIM
    for c in range(0, tm // CHUNK, 2):
        for j in range(D_SGU // LANES):
            cols = slice(LANES * j, LANES * (j + 1))
            blk = jnp.concatenate([vvn[CHUNK * c:CHUNK * (c + 1), cols],
                                   vvn[CHUNK * (c + 1):CHUNK * (c + 2), cols]], axis=1)
            r = jnp.dot(ws2_ref[j], blk, preferred_element_type=F32)
            for e in range(2):
                rows = slice(CHUNK * (c + e), CHUNK * (c + e + 1))
                r_e = r[:, LANES * e:LANES * (e + 1)]
                f = jnp.where(low_c, r_e[:CHUNK], r_e[CHUNK:]) + bias_ref[:, cols]
                gate_ref[rows, cols] = u[rows, cols] * f
    sgu_ref[...] = _rms_norm(gate_ref[...], gsout_ref[...]).astype(BF16)

    x1 = _swiglu_down(x, wd_ref, hid_ref)
    x1_ref[...] = x1
    x1_scr[...] = x1


def _attn_kernel(qt_ref, k_ref, vt_ref, kn2_ref, gout_ref, o_ref, w_ref, acc_ref):
    tq = qt_ref.shape[1]
    gcols = KV_GROUP * tq
    n_kv_blocks = k_ref.shape[0] // TK

    key_max2 = jnp.max(kn2_ref[...], axis=0, keepdims=True)
    shifts = []
    for g in range(N_KV_HEADS):
        q_norm2 = []
        for hh in range(KV_GROUP):
            h = KV_GROUP * g + hh
            q_h = qt_ref[HEAD_DIM * h:HEAD_DIM * (h + 1), :]
            w_ref[g, HEAD_DIM * g:HEAD_DIM * (g + 1), tq * hh:tq * (hh + 1)] = q_h
            q_f = q_h.astype(F32)
            q_norm2.append(jnp.sum(q_f * q_f, axis=0, keepdims=True))
        w_ref[g, HEAD_DIM * (1 - g):HEAD_DIM * (2 - g), :] = jnp.zeros((HEAD_DIM, gcols), BF16)
        k_max2 = jnp.max(key_max2[:, HEAD_DIM * g:HEAD_DIM * (g + 1)], axis=1, keepdims=True)
        shifts.append(jnp.sqrt(jnp.concatenate(q_norm2, axis=1) * k_max2) * SHIFT_MARGIN)
    acc_ref[...] = jnp.zeros(acc_ref.shape, F32)
    shift_is_safe = jnp.maximum(jnp.max(shifts[0]), jnp.max(shifts[1])) <= SAFE_SHIFT_LOG2

    def scores(i, g):
        off = pl.multiple_of(i * TK, TK)
        s = jnp.dot(k_ref[pl.ds(off, TK), :], w_ref[g], preferred_element_type=F32)
        vt = vt_ref[V_ROWS * g:V_ROWS * (g + 1), pl.ds(off, TK)]
        return s, vt

    @pl.when(shift_is_safe)
    def _():
        units = [(g, c, u) for g in range(N_KV_HEADS) for c in range(gcols // COL_CHUNK)
                 for u in range(KV_BLOCKS_PER_STEP)]

        def kv_step(i, carry):
            def key_rows(u):
                return pl.ds(pl.multiple_of((i * KV_BLOCKS_PER_STEP + u) * TK, TK), TK)

            def unit_scores(n):
                g, c, u = units[n]
                return jnp.dot(k_ref[key_rows(u), :], w_ref[g, :, COL_CHUNK * c:COL_CHUNK * (c + 1)],
                               preferred_element_type=F32)

            pending = [unit_scores(n) for n in range(QK_LOOKAHEAD)]
            pv = den = None
            for n, (g, c, u) in enumerate(units):
                s = pending.pop(0)
                if n + QK_LOOKAHEAD < len(units):
                    pending.append(unit_scores(n + QK_LOOKAHEAD))
                cols = slice(COL_CHUNK * c, COL_CHUNK * (c + 1))
                p = jnp.exp2(s - shifts[g][:, cols])
                part = jnp.sum(p.reshape(TK // F32_SUBLANES, F32_SUBLANES, COL_CHUNK), axis=0)
                vt = vt_ref[V_ROWS * g:V_ROWS * g + HEAD_DIM, key_rows(u)]
                d = jnp.dot(vt, p.astype(BF16), preferred_element_type=F32)
                pv, den = (d, part) if u == 0 else (pv + d, den + part)
                if u == KV_BLOCKS_PER_STEP - 1:
                    acc_ref[g, :HEAD_DIM, cols] += pv
                    acc_ref[g, HEAD_DIM:HEAD_DIM + F32_SUBLANES, cols] += den
            return carry
        lax.fori_loop(0, n_kv_blocks // KV_BLOCKS_PER_STEP, kv_step, 0)
        for g in range(N_KV_HEADS):
            den8 = acc_ref[g, HEAD_DIM:HEAD_DIM + F32_SUBLANES, :]
            acc_ref[g, HEAD_DIM:HEAD_DIM + 1, :] = jnp.sum(den8, axis=0, keepdims=True)

    @pl.when(jnp.logical_not(shift_is_safe))
    def _():
        def kv_step(i, ms):
            new_ms = []
            for g in range(N_KV_HEADS):
                s, vt = scores(i, g)
                m_new = jnp.maximum(ms[g], jnp.max(s, axis=0, keepdims=True))
                alpha = jnp.exp2(ms[g] - m_new)
                p = jnp.exp2(s - m_new).astype(BF16)
                acc_ref[g] = alpha * acc_ref[g] + jnp.dot(vt, p, preferred_element_type=F32)
                new_ms.append(m_new)
            return tuple(new_ms)
        m0 = jnp.full((1, gcols), -jnp.inf, F32)
        lax.fori_loop(0, n_kv_blocks, kv_step, (m0, m0))

    parts = []
    for g in range(N_KV_HEADS):
        a = acc_ref[g]
        o = a[:HEAD_DIM] / a[HEAD_DIM:HEAD_DIM + 1]
        parts.extend(o[:, tq * hh:tq * (hh + 1)] for hh in range(KV_GROUP))
    attn_t = jnp.concatenate(parts, axis=0)
    ms = jnp.mean(attn_t * attn_t, axis=0, keepdims=True)
    attn_n = (attn_t * lax.rsqrt(ms + EPS)).T * gout_ref[...]
    o_ref[...] = attn_n.astype(BF16)


def _back_kernel(x1_ref, attn_ref, sgu_ref, wo_a_ref, wo_s_ref, g2_ref, wg_ref, wu_ref, wd_ref,
                 gfin_ref, o_ref, hid_ref):
    mix = (jnp.dot(attn_ref[...], wo_a_ref[...], preferred_element_type=F32)
           + jnp.dot(sgu_ref[...], wo_s_ref[...], preferred_element_type=F32))
    x2 = x1_ref[...] + mix
    x3 = _swiglu_residual(x2, g2_ref, wg_ref, wu_ref, wd_ref, hid_ref)
    o_ref[...] = _rms_norm(x3, gfin_ref[...])


def _const_spec(shape):
    zeros = (0,) * len(shape)
    return pl.BlockSpec(shape, lambda *_: zeros, pipeline_mode=pl.Buffered(1))


def _rope_tables(seq):
    rows = seq // GRID_W
    row_idx = jnp.repeat(jnp.arange(rows, dtype=F32), GRID_W)
    col_idx = jnp.tile(jnp.arange(GRID_W, dtype=F32), rows)
    inv = 1.0 / (ROPE_THETA ** (jnp.arange(0, AXIS_DIM, 2, dtype=F32) / AXIS_DIM))
    ang = jnp.concatenate([row_idx[:, None] * inv, col_idx[:, None] * inv], axis=-1)
    cos = jnp.repeat(jnp.cos(ang), 2, axis=-1)
    sin = jnp.repeat(jnp.sin(ang), 2, axis=-1)
    sign = jnp.tile(jnp.array([-1.0, 1.0], F32), HEAD_DIM // 2)
    reps = LANES // HEAD_DIM
    return jnp.tile(cos, (1, reps)), jnp.tile(sin * sign, (1, reps))


def kernel(x, g_ffn1, w1_gate, w1_up, w1_down, g_mix, w_in, g_q, g_k, g_sgu, w_s, b_s,
           g_attn_out, g_sgu_out, w_out, g_ffn2, w2_gate, w2_up, w2_down, g_final):
    batch, seq, d_model = x.shape
    assert d_model == D_MODEL and seq % TM == 0 and seq % TK == 0 and seq % TQ == 0
    assert g_ffn1.shape[0] == 1, "single-layer block"
    n_tok = batch * seq
    tiles_per_seq = seq // TM

    row = lambda a: a.reshape(1, -1).astype(F32)
    mat = lambda w: w.reshape(w.shape[1:]).astype(BF16)
    wg1, wu1, wd1 = mat(w1_gate), mat(w1_up), mat(w1_down)
    wg2, wu2, wd2 = mat(w2_gate), mat(w2_up), mat(w2_down)
    win = mat(w_in)
    wo = mat(w_out)
    wo_a, wo_s = wo[:D_ATTN], wo[D_ATTN:]
    gq = jnp.tile(row(g_q[0]), (1, N_HEADS))
    gk = jnp.tile(row(g_k[0]), (1, N_KV_HEADS))
    head_id = jnp.arange(MXU_DIM) // HEAD_DIM
    bmat = jnp.where(head_id[:, None] == head_id[None, :], 1.0 / HEAD_DIM, 0.0).astype(BF16)
    cos_t, sin_t = _rope_tables(seq)
    ws2 = w_s[0].reshape(N_SGU_GROUPS // 2, 2 * CHUNK, CHUNK).astype(BF16)
    bias = jnp.repeat(b_s[0].T, D_SGU // N_SGU_GROUPS, axis=1).astype(F32)

    params = pltpu.CompilerParams(dimension_semantics=("arbitrary",),
                                  vmem_limit_bytes=VMEM_LIMIT_BYTES)
    n_tiles = n_tok // TM
    tok_spec = lambda width: pl.BlockSpec((TM, width), lambda i: (i, 0))

    ffn_tile = lambda i: jnp.minimum(i, n_tiles - 1)
    proj_tile = lambda i: jnp.maximum(i - 1, 0)
    ffn_spec = pl.BlockSpec((TM, D_MODEL), lambda i: (ffn_tile(i), 0))
    proj_spec = lambda width: pl.BlockSpec((TM, width), lambda i: (proj_tile(i), 0))
    pos_spec = pl.BlockSpec((TM, LANES), lambda i: (proj_tile(i) % tiles_per_seq, 0))
    seq_major_spec = lambda rows: pl.BlockSpec(
        (None, rows, TM),
        lambda i: (proj_tile(i) // tiles_per_seq, 0, proj_tile(i) % tiles_per_seq))

    x1, qt, k, vt, kn2, sgu = pl.pallas_call(
        _front_kernel,
        grid=(n_tiles + 1,),
        in_specs=[
            ffn_spec,
            _const_spec((1, D_MODEL)), _const_spec((D_MODEL, D_FF)), _const_spec((D_MODEL, D_FF)),
            _const_spec((D_FF, D_MODEL)),
            _const_spec((1, D_MODEL)), _const_spec((D_MODEL, D_IN)),
            _const_spec((1, D_ATTN)), _const_spec((1, D_KV)), _const_spec((MXU_DIM, MXU_DIM)),
            pos_spec, pos_spec,
            _const_spec((1, D_SGU)), _const_spec((N_SGU_GROUPS // 2, 2 * CHUNK, CHUNK)),
            _const_spec((CHUNK, D_SGU)), _const_spec((1, D_SGU)),
        ],
        out_specs=[
            ffn_spec,
            seq_major_spec(D_ATTN),
            proj_spec(D_KV),
            seq_major_spec(N_KV_HEADS * V_ROWS),
            pl.BlockSpec((F32_SUBLANES, D_KV), lambda i: (proj_tile(i), 0)),
            proj_spec(D_SGU),
        ],
        out_shape=[
            jax.ShapeDtypeStruct((n_tok, D_MODEL), F32),
            jax.ShapeDtypeStruct((batch, D_ATTN, seq), BF16),
            jax.ShapeDtypeStruct((n_tok, D_KV), BF16),
            jax.ShapeDtypeStruct((batch, N_KV_HEADS * V_ROWS, seq), BF16),
            jax.ShapeDtypeStruct((n_tok // TM * F32_SUBLANES, D_KV), F32),
            jax.ShapeDtypeStruct((n_tok, D_SGU), BF16),
        ],
        scratch_shapes=[pltpu.VMEM((TM, D_FF), BF16), pltpu.VMEM((TM, D_SGU), F32),
                        pltpu.VMEM((TM, D_MODEL), F32)],
        compiler_params=params,
        name="front_ffn_proj",
    )(x.reshape(n_tok, D_MODEL), row(g_ffn1[0]), wg1, wu1, wd1, row(g_mix[0]), win, gq, gk, bmat,
      cos_t, sin_t, row(g_sgu[0]), ws2, bias, row(g_sgu_out[0]))

    attn = pl.pallas_call(
        _attn_kernel,
        grid=(batch, seq // TQ),
        in_specs=[
            pl.BlockSpec((None, D_ATTN, TQ), lambda b, i: (b, 0, i)),
            pl.BlockSpec((None, seq, D_KV), lambda b, i: (b, 0, 0)),
            pl.BlockSpec((None, N_KV_HEADS * V_ROWS, seq), lambda b, i: (b, 0, 0)),
            pl.BlockSpec((None, tiles_per_seq * F32_SUBLANES, D_KV), lambda b, i: (b, 0, 0)),
            _const_spec((1, D_ATTN)),
        ],
        out_specs=pl.BlockSpec((None, TQ, D_ATTN), lambda b, i: (b, i, 0)),
        out_shape=jax.ShapeDtypeStruct((batch, seq, D_ATTN), BF16),
        scratch_shapes=[
            pltpu.VMEM((N_KV_HEADS, 2 * HEAD_DIM, KV_GROUP * TQ), BF16),
            pltpu.VMEM((N_KV_HEADS, V_ROWS, KV_GROUP * TQ), F32),
        ],
        compiler_params=pltpu.CompilerParams(dimension_semantics=("arbitrary", "arbitrary"),
                                             vmem_limit_bytes=VMEM_LIMIT_BYTES),
        name="gqa_attention",
    )(qt, k.reshape(batch, seq, D_KV), vt,
      kn2.reshape(batch, tiles_per_seq * F32_SUBLANES, D_KV), row(g_attn_out[0]))

    out = pl.pallas_call(
        _back_kernel,
        grid=(n_tok // TM,),
        in_specs=[
            tok_spec(D_MODEL), tok_spec(D_ATTN), tok_spec(D_SGU),
            _const_spec((D_ATTN, D_MODEL)), _const_spec((D_SGU, D_MODEL)),
            _const_spec((1, D_MODEL)), _const_spec((D_MODEL, D_FF)), _const_spec((D_MODEL, D_FF)),
            _const_spec((D_FF, D_MODEL)),
            _const_spec((1, D_MODEL)),
        ],
        out_specs=tok_spec(D_MODEL),
        out_shape=jax.ShapeDtypeStruct((n_tok, D_MODEL), F32),
        scratch_shapes=[pltpu.VMEM((TM, D_FF), BF16)],
        compiler_params=params,
        name="back_proj_ffn",
    )(x1, attn.reshape(n_tok, D_ATTN), sgu, wo_a, wo_s, row(g_ffn2[0]), wg2, wu2, wd2,
      row(g_final[0]))

    return out.reshape(batch, seq, D_MODEL)
```

```python
import functools

import jax
import jax.numpy as jnp
from jax import lax
from jax.experimental import pallas as pl
from jax.experimental.pallas import tpu as pltpu

F32 = jnp.float32
BF16 = jnp.bfloat16

D_MODEL = 1024
D_FF = 2816
N_HEADS = 8
HEAD_DIM = 64
N_KV_HEADS = 2
KV_GROUP = N_HEADS // N_KV_HEADS
D_ATTN = N_HEADS * HEAD_DIM
D_KV = N_KV_HEADS * HEAD_DIM
ROPE_THETA = 10000.0
AXIS_DIM = HEAD_DIM // 2
GRID_W = 64
N_SGU_GROUPS = 8
D_SGU = 512
CHUNK = 128
D_IN = D_ATTN + 2 * D_KV + 2 * D_SGU
EPS = 1e-6

LANES = 128
MXU_DIM = 256
VMEM_LIMIT_BYTES = 56 * 1024 * 1024

FF_CHUNK = MXU_DIM
N_FF_CHUNKS = D_FF // FF_CHUNK
TM = 512
TQ = 512
TK = 256
COL_CHUNK = 2 * MXU_DIM
QK_LOOKAHEAD = 2
LOG2_E = 1.4426950408889634
BOUND_MARGIN = 1.0 + 2.0 ** -6
SAFE_SCORE_LOG2 = 60.0
F32_SUBLANES = 8
BF16_SUBLANES = 16
V_ROWS = HEAD_DIM + BF16_SUBLANES


def _rms_norm(x, g):
    ms = jnp.mean(x * x, axis=-1, keepdims=True)
    return x * lax.rsqrt(ms + EPS) * g


def _swiglu_hidden(x, g_ref, wg_ref, wu_ref, hid_ref):
    xn = _rms_norm(x, g_ref[...]).astype(BF16)
    for c in range(N_FF_CHUNKS):
        cols = slice(FF_CHUNK * c, FF_CHUNK * (c + 1))
        gate = jnp.dot(xn, wg_ref[:, cols], preferred_element_type=F32)
        up = jnp.dot(xn, wu_ref[:, cols], preferred_element_type=F32)
        hid_ref[:, cols] = (jax.nn.silu(gate) * up).astype(BF16)


def _swiglu_down(x, wd_ref, hid_ref):
    return x + 0.5 * jnp.dot(hid_ref[...], wd_ref[...], preferred_element_type=F32)


def _swiglu_residual(x, g_ref, wg_ref, wu_ref, wd_ref, hid_ref):
    _swiglu_hidden(x, g_ref, wg_ref, wu_ref, hid_ref)
    return _swiglu_down(x, wd_ref, hid_ref)


def _head_mean_square(t, bmat):
    width = bmat.shape[0]
    t2 = (t * t).astype(BF16)
    slabs = [jnp.dot(t2[:, c:c + width], bmat, preferred_element_type=F32)
             for c in range(0, t.shape[1], width)]
    return slabs[0] if len(slabs) == 1 else jnp.concatenate(slabs, axis=1)


def _rope_block(t, cos_t, sin_t, even_lane):
    nxt = pltpu.roll(t, LANES - 1, axis=1)
    prv = pltpu.roll(t, 1, axis=1)
    return t * cos_t + jnp.where(even_lane, nxt, prv) * sin_t


def _front_kernel(x_ref, g1_ref, wg_ref, wu_ref, wd_ref, gmix_ref, win_ref, gq_ref, gk_ref,
                  bmat_ref, cos_ref, sin_ref, gsgu_ref, ws2_ref, bias_ref, gsout_ref,
                  x1_ref, qt_ref, k_ref, vt_ref, sgu_ref, hid_ref, gate_ref, x1_scr):
    tm = x_ref.shape[0]

    @pl.when(pl.program_id(0) == 0)
    def _():
        x1_scr[...] = jnp.zeros(x1_scr.shape, F32)

    h = _rms_norm(x1_scr[...], gmix_ref[...]).astype(BF16)
    proj = jnp.dot(h, win_ref[...], preferred_element_type=F32)
    x = x_ref[...]
    _swiglu_hidden(x, g1_ref, wg_ref, wu_ref, hid_ref)
    q = proj[:, :D_ATTN]
    k = proj[:, D_ATTN:D_ATTN + D_KV]
    v = proj[:, D_ATTN + D_KV:D_ATTN + 2 * D_KV]
    z = proj[:, D_ATTN + 2 * D_KV:]

    lane = lax.broadcasted_iota(jnp.int32, (tm, LANES), 1)
    even_lane = (lane % 2) == 0
    cos_t = cos_ref[...]
    sin_t = sin_ref[...]
    bmat = bmat_ref[...]

    qn = q * lax.rsqrt(_head_mean_square(q, bmat) + EPS) * gq_ref[...]
    scale = HEAD_DIM ** -0.5 * LOG2_E
    for j in range(D_ATTN // LANES):
        blk = _rope_block(qn[:, LANES * j:LANES * (j + 1)], cos_t, sin_t, even_lane)
        qt_ref[LANES * j:LANES * (j + 1), :] = (blk * scale).T.astype(BF16)

    kn = k * lax.rsqrt(_head_mean_square(k, bmat[:D_KV, :D_KV]) + EPS) * gk_ref[...]
    k_ref[...] = _rope_block(kn, cos_t, sin_t, even_lane).astype(BF16)

    v_t = v.T.astype(BF16)
    ones = jnp.ones((V_ROWS - HEAD_DIM, tm), BF16)
    for kv in range(N_KV_HEADS):
        vt_ref[V_ROWS * kv:V_ROWS * kv + HEAD_DIM, :] = v_t[HEAD_DIM * kv:HEAD_DIM * (kv + 1), :]
        vt_ref[V_ROWS * kv + HEAD_DIM:V_ROWS * (kv + 1), :] = ones

    zz = jax.nn.gelu(z)
    u = zz[:, :D_SGU]
    vvn = _rms_norm(zz[:, D_SGU:], gsgu_ref[...]).astype(BF16)
    low_c = lax.broadcasted_iota(jnp.int32, (CHUNK, LANES), 1) < HEAD_DIM
    for c in range(0, tm // CHUNK, 2):
        for j in range(D_SGU // LANES):
            cols = slice(LANES * j, LANES * (j + 1))
            blk = jnp.concatenate([vvn[CHUNK * c:CHUNK * (c + 1), cols],
                                   vvn[CHUNK * (c + 1):CHUNK * (c + 2), cols]], axis=1)
            r = jnp.dot(ws2_ref[j], blk, preferred_element_type=F32)
            for e in range(2):
                rows = slice(CHUNK * (c + e), CHUNK * (c + e + 1))
                r_e = r[:, LANES * e:LANES * (e + 1)]
                f = jnp.where(low_c, r_e[:CHUNK], r_e[CHUNK:]) + bias_ref[:, cols]
                gate_ref[rows, cols] = u[rows, cols] * f
    sgu_ref[...] = _rms_norm(gate_ref[...], gsout_ref[...]).astype(BF16)

    x1 = _swiglu_down(x, wd_ref, hid_ref)
    x1_ref[...] = x1
    x1_scr[...] = x1


def _attn_kernel(bound_ref, qt_ref, k_ref, vt_ref, gout_ref, o_ref, w_ref, acc_ref):
    tq = qt_ref.shape[1]
    gcols = KV_GROUP * tq
    n_kv_blocks = k_ref.shape[0] // TK

    for g in range(N_KV_HEADS):
        for hh in range(KV_GROUP):
            h = KV_GROUP * g + hh
            w_ref[g, HEAD_DIM * g:HEAD_DIM * (g + 1), tq * hh:tq * (hh + 1)] = (
                qt_ref[HEAD_DIM * h:HEAD_DIM * (h + 1), :])
        w_ref[g, HEAD_DIM * (1 - g):HEAD_DIM * (2 - g), :] = jnp.zeros((HEAD_DIM, gcols), BF16)
    shift_is_safe = bound_ref[0, 0] <= SAFE_SCORE_LOG2

    def scores(i, g):
        off = pl.multiple_of(i * TK, TK)
        s = jnp.dot(k_ref[pl.ds(off, TK), :], w_ref[g], preferred_element_type=F32)
        vt = vt_ref[V_ROWS * g:V_ROWS * (g + 1), pl.ds(off, TK)]
        return s, vt

    @pl.when(shift_is_safe)
    def _():
        units = [(g, c, u) for g in range(N_KV_HEADS) for c in range(gcols // COL_CHUNK)
                 for u in range(n_kv_blocks)]

        def unit_scores(n):
            g, c, u = units[n]
            return jnp.dot(k_ref[TK * u:TK * (u + 1), :],
                           w_ref[g, :, COL_CHUNK * c:COL_CHUNK * (c + 1)],
                           preferred_element_type=F32)

        pending = [unit_scores(n) for n in range(QK_LOOKAHEAD)]
        pv = den = None
        for n, (g, c, u) in enumerate(units):
            s = pending.pop(0)
            if n + QK_LOOKAHEAD < len(units):
                pending.append(unit_scores(n + QK_LOOKAHEAD))
            p = jnp.exp2(s)
            part = jnp.sum(p.reshape(TK // F32_SUBLANES, F32_SUBLANES, COL_CHUNK), axis=0)
            vt = vt_ref[V_ROWS * g:V_ROWS * g + HEAD_DIM, TK * u:TK * (u + 1)]
            d = jnp.dot(vt, p.astype(BF16), preferred_element_type=F32)
            pv, den = (d, part) if u == 0 else (pv + d, den + part)
            if u == n_kv_blocks - 1:
                cols = slice(COL_CHUNK * c, COL_CHUNK * (c + 1))
                acc_ref[g, :HEAD_DIM, cols] = pv
                acc_ref[g, HEAD_DIM:HEAD_DIM + 1, cols] = jnp.sum(den, axis=0, keepdims=True)

    @pl.when(jnp.logical_not(shift_is_safe))
    def _():
        acc_ref[...] = jnp.zeros(acc_ref.shape, F32)

        def kv_step(i, ms):
            new_ms = []
            for g in range(N_KV_HEADS):
                s, vt = scores(i, g)
                m_new = jnp.maximum(ms[g], jnp.max(s, axis=0, keepdims=True))
                alpha = jnp.exp2(ms[g] - m_new)
                p = jnp.exp2(s - m_new).astype(BF16)
                acc_ref[g] = alpha * acc_ref[g] + jnp.dot(vt, p, preferred_element_type=F32)
                new_ms.append(m_new)
            return tuple(new_ms)
        m0 = jnp.full((1, gcols), -jnp.inf, F32)
        lax.fori_loop(0, n_kv_blocks, kv_step, (m0, m0))

    parts = []
    for g in range(N_KV_HEADS):
        o = acc_ref[g, :HEAD_DIM, :] / acc_ref[g, HEAD_DIM:HEAD_DIM + 1, :]
        parts.extend(o[:, tq * hh:tq * (hh + 1)] for hh in range(KV_GROUP))
    attn_t = jnp.concatenate(parts, axis=0)
    ms = jnp.mean(attn_t * attn_t, axis=0, keepdims=True)
    attn_n = (attn_t * lax.rsqrt(ms + EPS)).T * gout_ref[...]
    o_ref[...] = attn_n.astype(BF16)


def _back_kernel(x1_ref, attn_ref, sgu_ref, wo_a_ref, wo_s_ref, g2_ref, wg_ref, wu_ref, wd_ref,
                 gfin_ref, o_ref, hid_ref):
    mix = (jnp.dot(attn_ref[...], wo_a_ref[...], preferred_element_type=F32)
           + jnp.dot(sgu_ref[...], wo_s_ref[...], preferred_element_type=F32))
    x2 = x1_ref[...] + mix
    x3 = _swiglu_residual(x2, g2_ref, wg_ref, wu_ref, wd_ref, hid_ref)
    o_ref[...] = _rms_norm(x3, gfin_ref[...])


def _const_spec(shape):
    zeros = (0,) * len(shape)
    return pl.BlockSpec(shape, lambda *_: zeros, pipeline_mode=pl.Buffered(1))


def _rope_tables(seq):
    rows = seq // GRID_W
    row_idx = jnp.repeat(jnp.arange(rows, dtype=F32), GRID_W)
    col_idx = jnp.tile(jnp.arange(GRID_W, dtype=F32), rows)
    inv = 1.0 / (ROPE_THETA ** (jnp.arange(0, AXIS_DIM, 2, dtype=F32) / AXIS_DIM))
    ang = jnp.concatenate([row_idx[:, None] * inv, col_idx[:, None] * inv], axis=-1)
    cos = jnp.repeat(jnp.cos(ang), 2, axis=-1)
    sin = jnp.repeat(jnp.sin(ang), 2, axis=-1)
    sign = jnp.tile(jnp.array([-1.0, 1.0], F32), HEAD_DIM // 2)
    reps = LANES // HEAD_DIM
    return jnp.tile(cos, (1, reps)), jnp.tile(sin * sign, (1, reps))


def kernel(x, g_ffn1, w1_gate, w1_up, w1_down, g_mix, w_in, g_q, g_k, g_sgu, w_s, b_s,
           g_attn_out, g_sgu_out, w_out, g_ffn2, w2_gate, w2_up, w2_down, g_final):
    batch, seq, d_model = x.shape
    assert d_model == D_MODEL and seq % TM == 0 and seq % TK == 0 and seq % TQ == 0
    assert g_ffn1.shape[0] == 1, "single-layer block"
    n_tok = batch * seq
    tiles_per_seq = seq // TM

    row = lambda a: a.reshape(1, -1).astype(F32)
    mat = lambda w: w.reshape(w.shape[1:]).astype(BF16)
    wg1, wu1, wd1 = mat(w1_gate), mat(w1_up), mat(w1_down)
    wg2, wu2, wd2 = mat(w2_gate), mat(w2_up), mat(w2_down)
    win = mat(w_in)
    wo = mat(w_out)
    wo_a, wo_s = wo[:D_ATTN], wo[D_ATTN:]
    gq = jnp.tile(row(g_q[0]), (1, N_HEADS))
    gk = jnp.tile(row(g_k[0]), (1, N_KV_HEADS))
    score_bound = (HEAD_DIM ** 0.5 * LOG2_E * BOUND_MARGIN
                   * jnp.max(jnp.abs(g_q[0])) * jnp.max(jnp.abs(g_k[0]))).reshape(1, 1).astype(F32)
    head_id = jnp.arange(MXU_DIM) // HEAD_DIM
    bmat = jnp.where(head_id[:, None] == head_id[None, :], 1.0 / HEAD_DIM, 0.0).astype(BF16)
    cos_t, sin_t = _rope_tables(seq)
    ws2 = w_s[0].reshape(N_SGU_GROUPS // 2, 2 * CHUNK, CHUNK).astype(BF16)
    bias = jnp.repeat(b_s[0].T, D_SGU // N_SGU_GROUPS, axis=1).astype(F32)

    params = pltpu.CompilerParams(dimension_semantics=("arbitrary",),
                                  vmem_limit_bytes=VMEM_LIMIT_BYTES)
    n_tiles = n_tok // TM
    tok_spec = lambda width: pl.BlockSpec((TM, width), lambda i: (i, 0))

    ffn_tile = lambda i: jnp.minimum(i, n_tiles - 1)
    proj_tile = lambda i: jnp.maximum(i - 1, 0)
    ffn_spec = pl.BlockSpec((TM, D_MODEL), lambda i: (ffn_tile(i), 0))
    proj_spec = lambda width: pl.BlockSpec((TM, width), lambda i: (proj_tile(i), 0))
    pos_spec = pl.BlockSpec((TM, LANES), lambda i: (proj_tile(i) % tiles_per_seq, 0))
    seq_major_spec = lambda rows: pl.BlockSpec(
        (None, rows, TM),
        lambda i: (proj_tile(i) // tiles_per_seq, 0, proj_tile(i) % tiles_per_seq))

    x1, qt, k, vt, sgu = pl.pallas_call(
        _front_kernel,
        grid=(n_tiles + 1,),
        in_specs=[
            ffn_spec,
            _const_spec((1, D_MODEL)), _const_spec((D_MODEL, D_FF)), _const_spec((D_MODEL, D_FF)),
            _const_spec((D_FF, D_MODEL)),
            _const_spec((1, D_MODEL)), _const_spec((D_MODEL, D_IN)),
            _const_spec((1, D_ATTN)), _const_spec((1, D_KV)), _const_spec((MXU_DIM, MXU_DIM)),
            pos_spec, pos_spec,
            _const_spec((1, D_SGU)), _const_spec((N_SGU_GROUPS // 2, 2 * CHUNK, CHUNK)),
            _const_spec((CHUNK, D_SGU)), _const_spec((1, D_SGU)),
        ],
        out_specs=[
            ffn_spec,
            seq_major_spec(D_ATTN),
            proj_spec(D_KV),
            seq_major_spec(N_KV_HEADS * V_ROWS),
            proj_spec(D_SGU),
        ],
        out_shape=[
            jax.ShapeDtypeStruct((n_tok, D_MODEL), F32),
            jax.ShapeDtypeStruct((batch, D_ATTN, seq), BF16),
            jax.ShapeDtypeStruct((n_tok, D_KV), BF16),
            jax.ShapeDtypeStruct((batch, N_KV_HEADS * V_ROWS, seq), BF16),
            jax.ShapeDtypeStruct((n_tok, D_SGU), BF16),
        ],
        scratch_shapes=[pltpu.VMEM((TM, D_FF), BF16), pltpu.VMEM((TM, D_SGU), F32),
                        pltpu.VMEM((TM, D_MODEL), F32)],
        compiler_params=params,
        name="front_ffn_proj",
    )(x.reshape(n_tok, D_MODEL), row(g_ffn1[0]), wg1, wu1, wd1, row(g_mix[0]), win, gq, gk, bmat,
      cos_t, sin_t, row(g_sgu[0]), ws2, bias, row(g_sgu_out[0]))

    attn = pl.pallas_call(
        _attn_kernel,
        grid=(batch, seq // TQ),
        in_specs=[
            pl.BlockSpec(memory_space=pltpu.SMEM),
            pl.BlockSpec((None, D_ATTN, TQ), lambda b, i: (b, 0, i)),
            pl.BlockSpec((None, seq, D_KV), lambda b, i: (b, 0, 0)),
            pl.BlockSpec((None, N_KV_HEADS * V_ROWS, seq), lambda b, i: (b, 0, 0)),
            _const_spec((1, D_ATTN)),
        ],
        out_specs=pl.BlockSpec((None, TQ, D_ATTN), lambda b, i: (b, i, 0)),
        out_shape=jax.ShapeDtypeStruct((batch, seq, D_ATTN), BF16),
        scratch_shapes=[
            pltpu.VMEM((N_KV_HEADS, 2 * HEAD_DIM, KV_GROUP * TQ), BF16),
            pltpu.VMEM((N_KV_HEADS, V_ROWS, KV_GROUP * TQ), F32),
        ],
        compiler_params=pltpu.CompilerParams(dimension_semantics=("arbitrary", "arbitrary"),
                                             vmem_limit_bytes=VMEM_LIMIT_BYTES),
        name="gqa_attention",
    )(score_bound, qt, k.reshape(batch, seq, D_KV), vt, row(g_attn_out[0]))

    out = pl.pallas_call(
        _back_kernel,
        grid=(n_tok // TM,),
        in_specs=[
            tok_spec(D_MODEL), tok_spec(D_ATTN), tok_spec(D_SGU),
            _const_spec((D_ATTN, D_MODEL)), _const_spec((D_SGU, D_MODEL)),
            _const_spec((1, D_MODEL)), _const_spec((D_MODEL, D_FF)), _const_spec((D_MODEL, D_FF)),
            _const_spec((D_FF, D_MODEL)),
            _const_spec((1, D_MODEL)),
        ],
        out_specs=tok_spec(D_MODEL),
        out_shape=jax.ShapeDtypeStruct((n_tok, D_MODEL), F32),
        scratch_shapes=[pltpu.VMEM((TM, D_FF), BF16)],
        compiler_params=params,
        name="back_proj_ffn",
    )(x1, attn.reshape(n_tok, D_ATTN), sgu, wo_a, wo_s, row(g_ffn2[0]), wg2, wu2, wd2,
      row(g_final[0]))

    return out.reshape(batch, seq, D_MODEL)
```

```python
import functools

import jax
import jax.numpy as jnp
from jax import lax
from jax.experimental import pallas as pl
from jax.experimental.pallas import tpu as pltpu

F32 = jnp.float32
BF16 = jnp.bfloat16

D_MODEL = 1024
D_FF = 2816
N_HEADS = 8
HEAD_DIM = 64
N_KV_HEADS = 2
KV_GROUP = N_HEADS // N_KV_HEADS
D_ATTN = N_HEADS * HEAD_DIM
D_KV = N_KV_HEADS * HEAD_DIM
ROPE_THETA = 10000.0
AXIS_DIM = HEAD_DIM // 2
GRID_W = 64
N_SGU_GROUPS = 8
D_SGU = 512
CHUNK = 128
D_IN = D_ATTN + 2 * D_KV + 2 * D_SGU
EPS = 1e-6

LANES = 128
MXU_DIM = 256
VMEM_LIMIT_BYTES = 56 * 1024 * 1024

FF_CHUNK = MXU_DIM
N_FF_CHUNKS = D_FF // FF_CHUNK
TM = 512
TM_BACK = 1024
TQ = 512
TK = 256
COL_CHUNK = 2 * MXU_DIM
QK_LOOKAHEAD = 2
LOG2_E = 1.4426950408889634
BOUND_MARGIN = 1.0 + 2.0 ** -6
SAFE_SCORE_LOG2 = 60.0
F32_SUBLANES = 8
BF16_SUBLANES = 16
V_ROWS = HEAD_DIM + BF16_SUBLANES


def _rms_norm(x, g):
    ms = jnp.mean(x * x, axis=-1, keepdims=True)
    return x * lax.rsqrt(ms + EPS) * g


def _swiglu_hidden(x, g_ref, wg_ref, wu_ref, hid_ref):
    xn = _rms_norm(x, g_ref[...]).astype(BF16)
    for c in range(N_FF_CHUNKS):
        cols = slice(FF_CHUNK * c, FF_CHUNK * (c + 1))
        gate = jnp.dot(xn, wg_ref[:, cols], preferred_element_type=F32)
        up = jnp.dot(xn, wu_ref[:, cols], preferred_element_type=F32)
        hid_ref[:, cols] = (jax.nn.silu(gate) * up).astype(BF16)


def _swiglu_down(x, wd_ref, hid_ref):
    return x + 0.5 * jnp.dot(hid_ref[...], wd_ref[...], preferred_element_type=F32)


def _swiglu_residual(x, g_ref, wg_ref, wu_ref, wd_ref, hid_ref):
    _swiglu_hidden(x, g_ref, wg_ref, wu_ref, hid_ref)
    return _swiglu_down(x, wd_ref, hid_ref)


def _head_mean_square(t, bmat):
    width = bmat.shape[0]
    t2 = (t * t).astype(BF16)
    slabs = [jnp.dot(t2[:, c:c + width], bmat, preferred_element_type=F32)
             for c in range(0, t.shape[1], width)]
    return slabs[0] if len(slabs) == 1 else jnp.concatenate(slabs, axis=1)


def _rope_block(t, cos_t, sin_t, even_lane):
    nxt = pltpu.roll(t, LANES - 1, axis=1)
    prv = pltpu.roll(t, 1, axis=1)
    return t * cos_t + jnp.where(even_lane, nxt, prv) * sin_t


def _front_kernel(x_ref, g1_ref, wg_ref, wu_ref, wd_ref, gmix_ref, win_ref, gq_ref, gk_ref,
                  bmat_ref, cos_ref, sin_ref, gsgu_ref, ws2_ref, bias_ref, gsout_ref,
                  x1_ref, qt_ref, k_ref, vt_ref, sgu_ref, hid_ref, gate_ref, x1_scr):
    tm = x_ref.shape[0]

    @pl.when(pl.program_id(0) == 0)
    def _():
        x1_scr[...] = jnp.zeros(x1_scr.shape, F32)

    h = _rms_norm(x1_scr[...], gmix_ref[...]).astype(BF16)
    proj = jnp.dot(h, win_ref[...], preferred_element_type=F32)
    x = x_ref[...]
    _swiglu_hidden(x, g1_ref, wg_ref, wu_ref, hid_ref)
    q = proj[:, :D_ATTN]
    k = proj[:, D_ATTN:D_ATTN + D_KV]
    v = proj[:, D_ATTN + D_KV:D_ATTN + 2 * D_KV]
    z = proj[:, D_ATTN + 2 * D_KV:]

    lane = lax.broadcasted_iota(jnp.int32, (tm, LANES), 1)
    even_lane = (lane % 2) == 0
    cos_t = cos_ref[...]
    sin_t = sin_ref[...]
    bmat = bmat_ref[...]

    qn = q * lax.rsqrt(_head_mean_square(q, bmat) + EPS) * gq_ref[...]
    scale = HEAD_DIM ** -0.5 * LOG2_E
    for j in range(D_ATTN // LANES):
        blk = _rope_block(qn[:, LANES * j:LANES * (j + 1)], cos_t, sin_t, even_lane)
        qt_ref[LANES * j:LANES * (j + 1), :] = (blk * scale).T.astype(BF16)

    kn = k * lax.rsqrt(_head_mean_square(k, bmat[:D_KV, :D_KV]) + EPS) * gk_ref[...]
    k_ref[...] = _rope_block(kn, cos_t, sin_t, even_lane).astype(BF16)

    v_t = v.T.astype(BF16)
    ones = jnp.ones((V_ROWS - HEAD_DIM, tm), BF16)
    for kv in range(N_KV_HEADS):
        vt_ref[V_ROWS * kv:V_ROWS * kv + HEAD_DIM, :] = v_t[HEAD_DIM * kv:HEAD_DIM * (kv + 1), :]
        vt_ref[V_ROWS * kv + HEAD_DIM:V_ROWS * (kv + 1), :] = ones

    zz = jax.nn.gelu(z)
    u = zz[:, :D_SGU]
    vvn = _rms_norm(zz[:, D_SGU:], gsgu_ref[...]).astype(BF16)
    low_c = lax.broadcasted_iota(jnp.int32, (CHUNK, LANES), 1) < HEAD_DIM
    for c in range(0, tm // CHUNK, 2):
        for j in range(D_SGU // LANES):
            cols = slice(LANES * j, LANES * (j + 1))
            blk = jnp.concatenate([vvn[CHUNK * c:CHUNK * (c + 1), cols],
                                   vvn[CHUNK * (c + 1):CHUNK * (c + 2), cols]], axis=1)
            r = jnp.dot(ws2_ref[j], blk, preferred_element_type=F32)
            for e in range(2):
                rows = slice(CHUNK * (c + e), CHUNK * (c + e + 1))
                r_e = r[:, LANES * e:LANES * (e + 1)]
                f = jnp.where(low_c, r_e[:CHUNK], r_e[CHUNK:]) + bias_ref[:, cols]
                gate_ref[rows, cols] = u[rows, cols] * f
    sgu_ref[...] = _rms_norm(gate_ref[...], gsout_ref[...]).astype(BF16)

    x1 = _swiglu_down(x, wd_ref, hid_ref)
    x1_ref[...] = x1
    x1_scr[...] = x1


def _attn_kernel(bound_ref, qt_ref, k_ref, vt_ref, gout_ref, o_ref, w_ref, acc_ref):
    tq = qt_ref.shape[1]
    gcols = KV_GROUP * tq
    n_kv_blocks = k_ref.shape[0] // TK

    for g in range(N_KV_HEADS):
        for hh in range(KV_GROUP):
            h = KV_GROUP * g + hh
            w_ref[g, HEAD_DIM * g:HEAD_DIM * (g + 1), tq * hh:tq * (hh + 1)] = (
                qt_ref[HEAD_DIM * h:HEAD_DIM * (h + 1), :])
        w_ref[g, HEAD_DIM * (1 - g):HEAD_DIM * (2 - g), :] = jnp.zeros((HEAD_DIM, gcols), BF16)
    shift_is_safe = bound_ref[0, 0] <= SAFE_SCORE_LOG2

    def scores(i, g):
        off = pl.multiple_of(i * TK, TK)
        s = jnp.dot(k_ref[pl.ds(off, TK), :], w_ref[g], preferred_element_type=F32)
        vt = vt_ref[V_ROWS * g:V_ROWS * (g + 1), pl.ds(off, TK)]
        return s, vt

    @pl.when(shift_is_safe)
    def _():
        units = [(g, c, u) for g in range(N_KV_HEADS) for c in range(gcols // COL_CHUNK)
                 for u in range(n_kv_blocks)]

        def unit_scores(n):
            g, c, u = units[n]
            return jnp.dot(k_ref[TK * u:TK * (u + 1), :],
                           w_ref[g, :, COL_CHUNK * c:COL_CHUNK * (c + 1)],
                           preferred_element_type=F32)

        pending = [unit_scores(n) for n in range(QK_LOOKAHEAD)]
        pv = den = None
        for n, (g, c, u) in enumerate(units):
            s = pending.pop(0)
            if n + QK_LOOKAHEAD < len(units):
                pending.append(unit_scores(n + QK_LOOKAHEAD))
            p = jnp.exp2(s)
            part = jnp.sum(p.reshape(TK // F32_SUBLANES, F32_SUBLANES, COL_CHUNK), axis=0)
            vt = vt_ref[V_ROWS * g:V_ROWS * g + HEAD_DIM, TK * u:TK * (u + 1)]
            d = jnp.dot(vt, p.astype(BF16), preferred_element_type=F32)
            pv, den = (d, part) if u == 0 else (pv + d, den + part)
            if u == n_kv_blocks - 1:
                cols = slice(COL_CHUNK * c, COL_CHUNK * (c + 1))
                acc_ref[g, :HEAD_DIM, cols] = pv
                acc_ref[g, HEAD_DIM:HEAD_DIM + 1, cols] = jnp.sum(den, axis=0, keepdims=True)

    @pl.when(jnp.logical_not(shift_is_safe))
    def _():
        acc_ref[...] = jnp.zeros(acc_ref.shape, F32)

        def kv_step(i, ms):
            new_ms = []
            for g in range(N_KV_HEADS):
                s, vt = scores(i, g)
                m_new = jnp.maximum(ms[g], jnp.max(s, axis=0, keepdims=True))
                alpha = jnp.exp2(ms[g] - m_new)
                p = jnp.exp2(s - m_new).astype(BF16)
                acc_ref[g] = alpha * acc_ref[g] + jnp.dot(vt, p, preferred_element_type=F32)
                new_ms.append(m_new)
            return tuple(new_ms)
        m0 = jnp.full((1, gcols), -jnp.inf, F32)
        lax.fori_loop(0, n_kv_blocks, kv_step, (m0, m0))

    parts = []
    for g in range(N_KV_HEADS):
        o = acc_ref[g, :HEAD_DIM, :] / acc_ref[g, HEAD_DIM:HEAD_DIM + 1, :]
        parts.extend(o[:, tq * hh:tq * (hh + 1)] for hh in range(KV_GROUP))
    attn_t = jnp.concatenate(parts, axis=0)
    ms = jnp.mean(attn_t * attn_t, axis=0, keepdims=True)
    o_ref[...] = (attn_t * lax.rsqrt(ms + EPS) * gout_ref[...]).astype(BF16)


def _back_kernel(x1_ref, attn_ref, sgu_ref, wo_a_ref, wo_s_ref, g2_ref, wg_ref, wu_ref, wd_ref,
                 gfin_ref, o_ref, hid_ref):
    mix = (jnp.dot(sgu_ref[...], wo_s_ref[...], preferred_element_type=F32)
           + lax.dot_general(attn_ref[...], wo_a_ref[...], (((0,), (0,)), ((), ())),
                             preferred_element_type=F32))
    x2 = x1_ref[...] + mix
    x3 = _swiglu_residual(x2, g2_ref, wg_ref, wu_ref, wd_ref, hid_ref)
    o_ref[...] = _rms_norm(x3, gfin_ref[...])


def _const_spec(shape):
    zeros = (0,) * len(shape)
    return pl.BlockSpec(shape, lambda *_: zeros, pipeline_mode=pl.Buffered(1))


def _rope_tables(seq):
    rows = seq // GRID_W
    row_idx = jnp.repeat(jnp.arange(rows, dtype=F32), GRID_W)
    col_idx = jnp.tile(jnp.arange(GRID_W, dtype=F32), rows)
    inv = 1.0 / (ROPE_THETA ** (jnp.arange(0, AXIS_DIM, 2, dtype=F32) / AXIS_DIM))
    ang = jnp.concatenate([row_idx[:, None] * inv, col_idx[:, None] * inv], axis=-1)
    cos = jnp.repeat(jnp.cos(ang), 2, axis=-1)
    sin = jnp.repeat(jnp.sin(ang), 2, axis=-1)
    sign = jnp.tile(jnp.array([-1.0, 1.0], F32), HEAD_DIM // 2)
    reps = LANES // HEAD_DIM
    return jnp.tile(cos, (1, reps)), jnp.tile(sin * sign, (1, reps))


def kernel(x, g_ffn1, w1_gate, w1_up, w1_down, g_mix, w_in, g_q, g_k, g_sgu, w_s, b_s,
           g_attn_out, g_sgu_out, w_out, g_ffn2, w2_gate, w2_up, w2_down, g_final):
    batch, seq, d_model = x.shape
    assert d_model == D_MODEL and seq % TM == 0 and seq % TK == 0 and seq % TQ == 0
    assert g_ffn1.shape[0] == 1, "single-layer block"
    n_tok = batch * seq
    tiles_per_seq = seq // TM

    row = lambda a: a.reshape(1, -1).astype(F32)
    mat = lambda w: w.reshape(w.shape[1:]).astype(BF16)
    wg1, wu1, wd1 = mat(w1_gate), mat(w1_up), mat(w1_down)
    wg2, wu2, wd2 = mat(w2_gate), mat(w2_up), mat(w2_down)
    win = mat(w_in)
    wo = mat(w_out)
    wo_a, wo_s = wo[:D_ATTN], wo[D_ATTN:]
    gq = jnp.tile(row(g_q[0]), (1, N_HEADS))
    gk = jnp.tile(row(g_k[0]), (1, N_KV_HEADS))
    score_bound = (HEAD_DIM ** 0.5 * LOG2_E * BOUND_MARGIN
                   * jnp.max(jnp.abs(g_q[0])) * jnp.max(jnp.abs(g_k[0]))).reshape(1, 1).astype(F32)
    head_id = jnp.arange(MXU_DIM) // HEAD_DIM
    bmat = jnp.where(head_id[:, None] == head_id[None, :], 1.0 / HEAD_DIM, 0.0).astype(BF16)
    cos_t, sin_t = _rope_tables(seq)
    ws2 = w_s[0].reshape(N_SGU_GROUPS // 2, 2 * CHUNK, CHUNK).astype(BF16)
    bias = jnp.repeat(b_s[0].T, D_SGU // N_SGU_GROUPS, axis=1).astype(F32)

    params = pltpu.CompilerParams(dimension_semantics=("arbitrary",),
                                  vmem_limit_bytes=VMEM_LIMIT_BYTES)
    n_tiles = n_tok // TM
    tok_spec = lambda width: pl.BlockSpec((TM, width), lambda i: (i, 0))

    ffn_tile = lambda i: jnp.minimum(i, n_tiles - 1)
    proj_tile = lambda i: jnp.maximum(i - 1, 0)
    ffn_spec = pl.BlockSpec((TM, D_MODEL), lambda i: (ffn_tile(i), 0))
    proj_spec = lambda width: pl.BlockSpec((TM, width), lambda i: (proj_tile(i), 0))
    pos_spec = pl.BlockSpec((TM, LANES), lambda i: (proj_tile(i) % tiles_per_seq, 0))
    seq_major_spec = lambda rows: pl.BlockSpec(
        (None, rows, TM),
        lambda i: (proj_tile(i) // tiles_per_seq, 0, proj_tile(i) % tiles_per_seq))

    x1, qt, k, vt, sgu = pl.pallas_call(
        _front_kernel,
        grid=(n_tiles + 1,),
        in_specs=[
            ffn_spec,
            _const_spec((1, D_MODEL)), _const_spec((D_MODEL, D_FF)), _const_spec((D_MODEL, D_FF)),
            _const_spec((D_FF, D_MODEL)),
            _const_spec((1, D_MODEL)), _const_spec((D_MODEL, D_IN)),
            _const_spec((1, D_ATTN)), _const_spec((1, D_KV)), _const_spec((MXU_DIM, MXU_DIM)),
            pos_spec, pos_spec,
            _const_spec((1, D_SGU)), _const_spec((N_SGU_GROUPS // 2, 2 * CHUNK, CHUNK)),
            _const_spec((CHUNK, D_SGU)), _const_spec((1, D_SGU)),
        ],
        out_specs=[
            ffn_spec,
            seq_major_spec(D_ATTN),
            proj_spec(D_KV),
            seq_major_spec(N_KV_HEADS * V_ROWS),
            proj_spec(D_SGU),
        ],
        out_shape=[
            jax.ShapeDtypeStruct((n_tok, D_MODEL), F32),
            jax.ShapeDtypeStruct((batch, D_ATTN, seq), BF16),
            jax.ShapeDtypeStruct((n_tok, D_KV), BF16),
            jax.ShapeDtypeStruct((batch, N_KV_HEADS * V_ROWS, seq), BF16),
            jax.ShapeDtypeStruct((n_tok, D_SGU), BF16),
        ],
        scratch_shapes=[pltpu.VMEM((TM, D_FF), BF16), pltpu.VMEM((TM, D_SGU), F32),
                        pltpu.VMEM((TM, D_MODEL), F32)],
        compiler_params=params,
        name="front_ffn_proj",
    )(x.reshape(n_tok, D_MODEL), row(g_ffn1[0]), wg1, wu1, wd1, row(g_mix[0]), win, gq, gk, bmat,
      cos_t, sin_t, row(g_sgu[0]), ws2, bias, row(g_sgu_out[0]))

    attn = pl.pallas_call(
        _attn_kernel,
        grid=(batch, seq // TQ),
        in_specs=[
            pl.BlockSpec(memory_space=pltpu.SMEM),
            pl.BlockSpec((None, D_ATTN, TQ), lambda b, i: (b, 0, i)),
            pl.BlockSpec((None, seq, D_KV), lambda b, i: (b, 0, 0)),
            pl.BlockSpec((None, N_KV_HEADS * V_ROWS, seq), lambda b, i: (b, 0, 0)),
            _const_spec((D_ATTN, 1)),
        ],
        out_specs=pl.BlockSpec((None, D_ATTN, TQ), lambda b, i: (b, 0, i)),
        out_shape=jax.ShapeDtypeStruct((batch, D_ATTN, seq), BF16),
        scratch_shapes=[
            pltpu.VMEM((N_KV_HEADS, 2 * HEAD_DIM, KV_GROUP * TQ), BF16),
            pltpu.VMEM((N_KV_HEADS, V_ROWS, KV_GROUP * TQ), F32),
        ],
        compiler_params=pltpu.CompilerParams(dimension_semantics=("arbitrary", "arbitrary"),
                                             vmem_limit_bytes=VMEM_LIMIT_BYTES),
        name="gqa_attention",
    )(score_bound, qt, k.reshape(batch, seq, D_KV), vt,
      g_attn_out[0].reshape(D_ATTN, 1).astype(F32))

    back_tiles_per_seq = seq // TM_BACK
    tok_spec = lambda width: pl.BlockSpec((TM_BACK, width), lambda i: (i, 0))
    out = pl.pallas_call(
        _back_kernel,
        grid=(n_tok // TM_BACK,),
        in_specs=[
            tok_spec(D_MODEL),
            pl.BlockSpec((None, D_ATTN, TM_BACK),
                         lambda i: (i // back_tiles_per_seq, 0, i % back_tiles_per_seq)),
            tok_spec(D_SGU),
            _const_spec((D_ATTN, D_MODEL)), _const_spec((D_SGU, D_MODEL)),
            _const_spec((1, D_MODEL)), _const_spec((D_MODEL, D_FF)), _const_spec((D_MODEL, D_FF)),
            _const_spec((D_FF, D_MODEL)),
            _const_spec((1, D_MODEL)),
        ],
        out_specs=tok_spec(D_MODEL),
        out_shape=jax.ShapeDtypeStruct((n_tok, D_MODEL), F32),
        scratch_shapes=[pltpu.VMEM((TM_BACK, D_FF), BF16)],
        compiler_params=params,
        name="back_proj_ffn",
    )(x1, attn, sgu, wo_a, wo_s, row(g_ffn2[0]), wg2, wu2, wd2,
      row(g_final[0]))

    return out.reshape(batch, seq, D_MODEL)
```

```python
import jax
import jax.numpy as jnp
from jax import lax
from jax.experimental import pallas as pl
from jax.experimental.pallas import tpu as pltpu

F32 = jnp.float32
BF16 = jnp.bfloat16

D_MODEL = 1024
D_FF = 2816
N_HEADS = 8
HEAD_DIM = 64
N_KV_HEADS = 2
KV_GROUP = N_HEADS // N_KV_HEADS
D_ATTN = N_HEADS * HEAD_DIM
D_KV = N_KV_HEADS * HEAD_DIM
ROPE_THETA = 10000.0
AXIS_DIM = HEAD_DIM // 2
GRID_W = 64
N_SGU_GROUPS = 8
D_SGU = 512
CHUNK = 128
D_IN = D_ATTN + 2 * D_KV + 2 * D_SGU
EPS = 1e-6

LANES = 128
MXU_DIM = 256
VMEM_LIMIT_BYTES = 56 * 1024 * 1024

FF_CHUNK = MXU_DIM
N_FF_CHUNKS = D_FF // FF_CHUNK
TM = 512
TM_BACK = 1024
TQ = 512
TK = 256
COL_CHUNK = 2 * MXU_DIM
QK_LOOKAHEAD = 2
LOG2_E = 1.4426950408889634
BOUND_MARGIN = 1.0 + 2.0 ** -6
SAFE_SCORE_LOG2 = 60.0
F32_SUBLANES = 8
BF16_SUBLANES = 16
V_ROWS = HEAD_DIM + BF16_SUBLANES


def _rms_norm(x, g):
    ms = jnp.mean(x * x, axis=-1, keepdims=True)
    return x * lax.rsqrt(ms + EPS) * g


def _swiglu_hidden(x, g_ref, wg_ref, wu_ref, hid_ref):
    xn = _rms_norm(x, g_ref[...]).astype(BF16)
    for c in range(N_FF_CHUNKS):
        cols = slice(FF_CHUNK * c, FF_CHUNK * (c + 1))
        gate = jnp.dot(xn, wg_ref[:, cols], preferred_element_type=F32)
        up = jnp.dot(xn, wu_ref[:, cols], preferred_element_type=F32)
        hid_ref[:, cols] = (jax.nn.silu(gate) * up).astype(BF16)


def _swiglu_down(x, wd_ref, hid_ref):
    return x + 0.5 * jnp.dot(hid_ref[...], wd_ref[...], preferred_element_type=F32)


def _swiglu_residual(x, g_ref, wg_ref, wu_ref, wd_ref, hid_ref):
    _swiglu_hidden(x, g_ref, wg_ref, wu_ref, hid_ref)
    return _swiglu_down(x, wd_ref, hid_ref)


def _head_mean_square(t, bmat):
    width = bmat.shape[0]
    t2 = (t * t).astype(BF16)
    slabs = [jnp.dot(t2[:, c:c + width], bmat, preferred_element_type=F32)
             for c in range(0, t.shape[1], width)]
    return slabs[0] if len(slabs) == 1 else jnp.concatenate(slabs, axis=1)


def _rope_block(t, cos_t, sin_t, even_lane):
    nxt = pltpu.roll(t, LANES - 1, axis=1)
    prv = pltpu.roll(t, 1, axis=1)
    return t * cos_t + jnp.where(even_lane, nxt, prv) * sin_t


def _front_kernel(x_ref, g1_ref, wg_ref, wu_ref, wd_ref, gmix_ref, win_ref, gq_ref, gk_ref,
                  bmat_ref, cos_ref, sin_ref, gsgu_ref, ws2_ref, bias_ref, gsout_ref,
                  x1_ref, qt_ref, k_ref, vt_ref, sgu_ref, hid_ref, gate_ref, x1_scr):
    tm = x_ref.shape[0]

    @pl.when(pl.program_id(0) == 0)
    def _():
        x1_scr[...] = jnp.zeros(x1_scr.shape, F32)

    h = _rms_norm(x1_scr[...], gmix_ref[...]).astype(BF16)
    proj = jnp.dot(h, win_ref[...], preferred_element_type=F32)
    x = x_ref[...]
    _swiglu_hidden(x, g1_ref, wg_ref, wu_ref, hid_ref)
    q = proj[:, :D_ATTN]
    k = proj[:, D_ATTN:D_ATTN + D_KV]
    v = proj[:, D_ATTN + D_KV:D_ATTN + 2 * D_KV]
    z = proj[:, D_ATTN + 2 * D_KV:]

    lane = lax.broadcasted_iota(jnp.int32, (tm, LANES), 1)
    even_lane = (lane % 2) == 0
    cos_t = cos_ref[...]
    sin_t = sin_ref[...]
    bmat = bmat_ref[...]

    qn = q * lax.rsqrt(_head_mean_square(q, bmat) + EPS) * gq_ref[...]
    scale = HEAD_DIM ** -0.5 * LOG2_E
    for j in range(D_ATTN // LANES):
        blk = _rope_block(qn[:, LANES * j:LANES * (j + 1)], cos_t, sin_t, even_lane)
        qt_ref[LANES * j:LANES * (j + 1), :] = (blk * scale).T.astype(BF16)

    kn = k * lax.rsqrt(_head_mean_square(k, bmat[:D_KV, :D_KV]) + EPS) * gk_ref[...]
    k_ref[...] = _rope_block(kn, cos_t, sin_t, even_lane).astype(BF16)

    v_t = v.T.astype(BF16)
    ones = jnp.ones((V_ROWS - HEAD_DIM, tm), BF16)
    for kv in range(N_KV_HEADS):
        vt_ref[V_ROWS * kv:V_ROWS * kv + HEAD_DIM, :] = v_t[HEAD_DIM * kv:HEAD_DIM * (kv + 1), :]
        vt_ref[V_ROWS * kv + HEAD_DIM:V_ROWS * (kv + 1), :] = ones

    zz = jax.nn.gelu(z)
    u = zz[:, :D_SGU]
    vvn = _rms_norm(zz[:, D_SGU:], gsgu_ref[...]).astype(BF16)
    low_c = lax.broadcasted_iota(jnp.int32, (CHUNK, LANES), 1) < HEAD_DIM
    for c in range(0, tm // CHUNK, 2):
        for j in range(D_SGU // LANES):
            cols = slice(LANES * j, LANES * (j + 1))
            blk = jnp.concatenate([vvn[CHUNK * c:CHUNK * (c + 1), cols],
                                   vvn[CHUNK * (c + 1):CHUNK * (c + 2), cols]], axis=1)
            r = jnp.dot(ws2_ref[j], blk, preferred_element_type=F32)
            for e in range(2):
                rows = slice(CHUNK * (c + e), CHUNK * (c + e + 1))
                r_e = r[:, LANES * e:LANES * (e + 1)]
                f = jnp.where(low_c, r_e[:CHUNK], r_e[CHUNK:]) + bias_ref[:, cols]
                gate_ref[rows, cols] = u[rows, cols] * f
    sgu_ref[...] = _rms_norm(gate_ref[...], gsout_ref[...]).astype(BF16)

    x1 = _swiglu_down(x, wd_ref, hid_ref)
    x1_ref[...] = x1
    x1_scr[...] = x1


def _attn_kernel(bound_ref, qt_ref, k_ref, vt_ref, gout_ref, o_ref, w_ref, acc_ref):
    tq = qt_ref.shape[1]
    gcols = KV_GROUP * tq
    n_kv_blocks = k_ref.shape[0] // TK

    def build_rhs():
        for g in range(N_KV_HEADS):
            for hh in range(KV_GROUP):
                h = KV_GROUP * g + hh
                w_ref[g, HEAD_DIM * g:HEAD_DIM * (g + 1), tq * hh:tq * (hh + 1)] = (
                    qt_ref[HEAD_DIM * h:HEAD_DIM * (h + 1), :])
            w_ref[g, HEAD_DIM * (1 - g):HEAD_DIM * (2 - g), :] = jnp.zeros((HEAD_DIM, gcols), BF16)

    def finish(normalized):
        parts = []
        for g in range(N_KV_HEADS):
            o = acc_ref[g, :HEAD_DIM, :]
            if not normalized:
                o = o / acc_ref[g, HEAD_DIM:HEAD_DIM + 1, :]
            parts.extend(o[:, tq * hh:tq * (hh + 1)] for hh in range(KV_GROUP))
        attn_t = jnp.concatenate(parts, axis=0)
        ms = jnp.mean(attn_t * attn_t, axis=0, keepdims=True)
        o_ref[...] = (attn_t * lax.rsqrt(ms + EPS) * gout_ref[...]).astype(BF16)

    shift_is_safe = bound_ref[0, 0] <= SAFE_SCORE_LOG2

    def scores(i, g):
        off = pl.multiple_of(i * TK, TK)
        s = jnp.dot(k_ref[pl.ds(off, TK), :], w_ref[g], preferred_element_type=F32)
        vt = vt_ref[V_ROWS * g:V_ROWS * (g + 1), pl.ds(off, TK)]
        return s, vt

    @pl.when(shift_is_safe)
    def _():
        build_rhs()
        units =[(g, c, u) for g in range(N_KV_HEADS) for c in range(gcols // COL_CHUNK)
                 for u in range(n_kv_blocks)]

        def unit_scores(n):
            g, c, u = units[n]
            return jnp.dot(k_ref[TK * u:TK * (u + 1), :],
                           w_ref[g, :, COL_CHUNK * c:COL_CHUNK * (c + 1)],
                           preferred_element_type=F32)

        pending = [unit_scores(n) for n in range(QK_LOOKAHEAD)]
        pv = den = None
        for n, (g, c, u) in enumerate(units):
            s = pending.pop(0)
            if n + QK_LOOKAHEAD < len(units):
                pending.append(unit_scores(n + QK_LOOKAHEAD))
            p = jnp.exp2(s)
            part = jnp.sum(p.reshape(TK // F32_SUBLANES, F32_SUBLANES, COL_CHUNK), axis=0)
            vt = vt_ref[V_ROWS * g:V_ROWS * g + HEAD_DIM, TK * u:TK * (u + 1)]
            d = jnp.dot(vt, p.astype(BF16), preferred_element_type=F32)
            pv, den = (d, part) if u == 0 else (pv + d, den + part)
            if u == n_kv_blocks - 1:
                cols = slice(COL_CHUNK * c, COL_CHUNK * (c + 1))
                acc_ref[g, :HEAD_DIM, cols] = pv / jnp.sum(den, axis=0, keepdims=True)
        finish(normalized=True)

    @pl.when(jnp.logical_not(shift_is_safe))
    def _():
        build_rhs()
        acc_ref[...] = jnp.zeros(acc_ref.shape, F32)

        def kv_step(i, ms):
            new_ms = []
            for g in range(N_KV_HEADS):
                s, vt = scores(i, g)
                m_new = jnp.maximum(ms[g], jnp.max(s, axis=0, keepdims=True))
                alpha = jnp.exp2(ms[g] - m_new)
                p = jnp.exp2(s - m_new).astype(BF16)
                acc_ref[g] = alpha * acc_ref[g] + jnp.dot(vt, p, preferred_element_type=F32)
                new_ms.append(m_new)
            return tuple(new_ms)
        m0 = jnp.full((1, gcols), -jnp.inf, F32)
        lax.fori_loop(0, n_kv_blocks, kv_step, (m0, m0))
        finish(normalized=False)


def _back_kernel(x1_ref, attn_ref, sgu_ref, wo_a_ref, wo_s_ref, g2_ref, wg_ref, wu_ref, wd_ref,
                 gfin_ref, o_ref, hid_ref):
    mix = (jnp.dot(sgu_ref[...], wo_s_ref[...], preferred_element_type=F32)
           + lax.dot_general(attn_ref[...], wo_a_ref[...], (((0,), (0,)), ((), ())),
                             preferred_element_type=F32))
    x2 = x1_ref[...] + mix
    x3 = _swiglu_residual(x2, g2_ref, wg_ref, wu_ref, wd_ref, hid_ref)
    o_ref[...] = _rms_norm(x3, gfin_ref[...])


def _const_spec(shape):
    zeros = (0,) * len(shape)
    return pl.BlockSpec(shape, lambda *_: zeros, pipeline_mode=pl.Buffered(1))


def _rope_tables(seq):
    rows = seq // GRID_W
    row_idx = jnp.repeat(jnp.arange(rows, dtype=F32), GRID_W)
    col_idx = jnp.tile(jnp.arange(GRID_W, dtype=F32), rows)
    inv = 1.0 / (ROPE_THETA ** (jnp.arange(0, AXIS_DIM, 2, dtype=F32) / AXIS_DIM))
    ang = jnp.concatenate([row_idx[:, None] * inv, col_idx[:, None] * inv], axis=-1)
    cos = jnp.repeat(jnp.cos(ang), 2, axis=-1)
    sin = jnp.repeat(jnp.sin(ang), 2, axis=-1)
    sign = jnp.tile(jnp.array([-1.0, 1.0], F32), HEAD_DIM // 2)
    reps = LANES // HEAD_DIM
    return jnp.tile(cos, (1, reps)), jnp.tile(sin * sign, (1, reps))


def kernel(x, g_ffn1, w1_gate, w1_up, w1_down, g_mix, w_in, g_q, g_k, g_sgu, w_s, b_s,
           g_attn_out, g_sgu_out, w_out, g_ffn2, w2_gate, w2_up, w2_down, g_final):
    batch, seq, d_model = x.shape
    assert d_model == D_MODEL and seq % TM == 0 and seq % TK == 0 and seq % TQ == 0
    assert g_ffn1.shape[0] == 1, "single-layer block"
    n_tok = batch * seq
    tiles_per_seq = seq // TM

    row = lambda a: a.reshape(1, -1).astype(F32)
    mat = lambda w: w.reshape(w.shape[1:]).astype(BF16)
    wg1, wu1, wd1 = mat(w1_gate), mat(w1_up), mat(w1_down)
    wg2, wu2, wd2 = mat(w2_gate), mat(w2_up), mat(w2_down)
    win = mat(w_in)
    wo = mat(w_out)
    wo_a, wo_s = wo[:D_ATTN], wo[D_ATTN:]
    gq = jnp.tile(row(g_q[0]), (1, N_HEADS))
    gk = jnp.tile(row(g_k[0]), (1, N_KV_HEADS))
    score_bound = (HEAD_DIM ** 0.5 * LOG2_E * BOUND_MARGIN
                   * jnp.max(jnp.abs(g_q[0])) * jnp.max(jnp.abs(g_k[0]))).reshape(1, 1).astype(F32)
    head_id = jnp.arange(MXU_DIM) // HEAD_DIM
    bmat = jnp.where(head_id[:, None] == head_id[None, :], 1.0 / HEAD_DIM, 0.0).astype(BF16)
    cos_t, sin_t = _rope_tables(seq)
    ws2 = w_s[0].reshape(N_SGU_GROUPS // 2, 2 * CHUNK, CHUNK).astype(BF16)
    bias = jnp.repeat(b_s[0].T, D_SGU // N_SGU_GROUPS, axis=1).astype(F32)

    params = pltpu.CompilerParams(dimension_semantics=("arbitrary",),
                                  vmem_limit_bytes=VMEM_LIMIT_BYTES)
    n_tiles = n_tok // TM
    tok_spec = lambda width: pl.BlockSpec((TM, width), lambda i: (i, 0))

    ffn_tile = lambda i: jnp.minimum(i, n_tiles - 1)
    proj_tile = lambda i: jnp.maximum(i - 1, 0)
    ffn_spec = pl.BlockSpec((TM, D_MODEL), lambda i: (ffn_tile(i), 0))
    proj_spec = lambda width: pl.BlockSpec((TM, width), lambda i: (proj_tile(i), 0))
    pos_spec = pl.BlockSpec((TM, LANES), lambda i: (proj_tile(i) % tiles_per_seq, 0))
    seq_major_spec = lambda rows: pl.BlockSpec(
        (None, rows, TM),
        lambda i: (proj_tile(i) // tiles_per_seq, 0, proj_tile(i) % tiles_per_seq))

    x1, qt, k, vt, sgu = pl.pallas_call(
        _front_kernel,
        grid=(n_tiles + 1,),
        in_specs=[
            ffn_spec,
            _const_spec((1, D_MODEL)), _const_spec((D_MODEL, D_FF)), _const_spec((D_MODEL, D_FF)),
            _const_spec((D_FF, D_MODEL)),
            _const_spec((1, D_MODEL)), _const_spec((D_MODEL, D_IN)),
            _const_spec((1, D_ATTN)), _const_spec((1, D_KV)), _const_spec((MXU_DIM, MXU_DIM)),
            pos_spec, pos_spec,
            _const_spec((1, D_SGU)), _const_spec((N_SGU_GROUPS // 2, 2 * CHUNK, CHUNK)),
            _const_spec((CHUNK, D_SGU)), _const_spec((1, D_SGU)),
        ],
        out_specs=[
            ffn_spec,
            seq_major_spec(D_ATTN),
            proj_spec(D_KV),
            seq_major_spec(N_KV_HEADS * V_ROWS),
            proj_spec(D_SGU),
        ],
        out_shape=[
            jax.ShapeDtypeStruct((n_tok, D_MODEL), F32),
            jax.ShapeDtypeStruct((batch, D_ATTN, seq), BF16),
            jax.ShapeDtypeStruct((n_tok, D_KV), BF16),
            jax.ShapeDtypeStruct((batch, N_KV_HEADS * V_ROWS, seq), BF16),
            jax.ShapeDtypeStruct((n_tok, D_SGU), BF16),
        ],
        scratch_shapes=[pltpu.VMEM((TM, D_FF), BF16), pltpu.VMEM((TM, D_SGU), F32),
                        pltpu.VMEM((TM, D_MODEL), F32)],
        compiler_params=params,
        name="front_ffn_proj",
    )(x.reshape(n_tok, D_MODEL), row(g_ffn1[0]), wg1, wu1, wd1, row(g_mix[0]), win, gq, gk, bmat,
      cos_t, sin_t, row(g_sgu[0]), ws2, bias, row(g_sgu_out[0]))

    attn = pl.pallas_call(
        _attn_kernel,
        grid=(batch, seq // TQ),
        in_specs=[
            pl.BlockSpec(memory_space=pltpu.SMEM),
            pl.BlockSpec((None, D_ATTN, TQ), lambda b, i: (b, 0, i)),
            pl.BlockSpec((None, seq, D_KV), lambda b, i: (b, 0, 0)),
            pl.BlockSpec((None, N_KV_HEADS * V_ROWS, seq), lambda b, i: (b, 0, 0)),
            _const_spec((D_ATTN, 1)),
        ],
        out_specs=pl.BlockSpec((None, D_ATTN, TQ), lambda b, i: (b, 0, i)),
        out_shape=jax.ShapeDtypeStruct((batch, D_ATTN, seq), BF16),
        scratch_shapes=[
            pltpu.VMEM((N_KV_HEADS, 2 * HEAD_DIM, KV_GROUP * TQ), BF16),
            pltpu.VMEM((N_KV_HEADS, V_ROWS, KV_GROUP * TQ), F32),
        ],
        compiler_params=pltpu.CompilerParams(dimension_semantics=("arbitrary", "arbitrary"),
                                             vmem_limit_bytes=VMEM_LIMIT_BYTES),
        name="gqa_attention",
    )(score_bound, qt, k.reshape(batch, seq, D_KV), vt,
      g_attn_out[0].reshape(D_ATTN, 1).astype(F32))

    back_tiles_per_seq = seq // TM_BACK
    tok_spec = lambda width: pl.BlockSpec((TM_BACK, width), lambda i: (i, 0))
    out = pl.pallas_call(
        _back_kernel,
        grid=(n_tok // TM_BACK,),
        in_specs=[
            tok_spec(D_MODEL),
            pl.BlockSpec((None, D_ATTN, TM_BACK),
                         lambda i: (i // back_tiles_per_seq, 0, i % back_tiles_per_seq)),
            tok_spec(D_SGU),
            _const_spec((D_ATTN, D_MODEL)), _const_spec((D_SGU, D_MODEL)),
            _const_spec((1, D_MODEL)), _const_spec((D_MODEL, D_FF)), _const_spec((D_MODEL, D_FF)),
            _const_spec((D_FF, D_MODEL)),
            _const_spec((1, D_MODEL)),
        ],
        out_specs=tok_spec(D_MODEL),
        out_shape=jax.ShapeDtypeStruct((n_tok, D_MODEL), F32),
        scratch_shapes=[pltpu.VMEM((TM_BACK, D_FF), BF16)],
        compiler_params=params,
        name="back_proj_ffn",
    )(x1, attn, sgu, wo_a, wo_s, row(g_ffn2[0]), wg2, wu2, wd2,
      row(g_final[0]))

    return out.reshape(batch, seq, D_MODEL)
```
